```python
import math
import jax
import jax.numpy as jnp
from jax import lax
import numpy as np

D_MODEL = 1024
BATCH = 8
SEQ = 2048
DEPTH = 4
DEC_BATCH = 128
DEC_SEQ = 4
PAST_LEN = 2048
PAGE_SIZE = 128

HEAD_DIM = 64
ROT_DIM = HEAD_DIM // 4
ROPE_THETA = 500000.0
SB_HEADS = D_MODEL // (2 * HEAD_DIM)
DIFF_HEADS = D_MODEL // (4 * HEAD_DIM)
AB_KV_HEADS = SB_HEADS + 2 * DIFF_HEADS
SB_W = SB_HEADS * HEAD_DIM
DF_W = DIFF_HEADS * 2 * HEAD_DIM
AB_IN = 3 * SB_W + 3 * DF_W
AB_OUT = SB_W + DF_W
NSA_HEADS = D_MODEL // HEAD_DIM
NSA_GROUPS = 2
NSA_IN = NSA_HEADS * HEAD_DIM + 6 * NSA_GROUPS * HEAD_DIM + 3 * NSA_HEADS
BLK = 64
TOPK = 8
WINDOW = 256
MEM_LEN = 256
MEM_HEADS = 4
MEM_HEAD_DIM = D_MODEL // 8
MEM_W = MEM_HEADS * MEM_HEAD_DIM
D_FF = ((8 * D_MODEL // 3 + 127) // 128) * 128
QBLK = 128
N_EVEN = (DEPTH + 1) // 2
N_ODD = DEPTH // 2
EPS = 1e-6
NEG = -1e30
FORCE = 1e4

kernel_name = 'hybrid_sb_diff_nsa_macaron_step'


def rms_norm(x, g):
    xf = x.astype(jnp.float32)
    y = xf * lax.rsqrt(jnp.mean(xf * xf, axis=-1, keepdims=True) + EPS)
    return (y * g.astype(jnp.float32)).astype(x.dtype)


def swiglu(x, w_in, w_out):
    gate, up = jnp.split(x @ w_in, 2, axis=-1)
    return (jax.nn.silu(gate) * up) @ w_out


def ffn_half(x, g_pre, g_post, w_in, w_out):
    return x + 0.5 * rms_norm(swiglu(rms_norm(x, g_pre), w_in, w_out), g_post)


def rope(x, pos):
    half = ROT_DIM // 2
    inv = jnp.power(ROPE_THETA, -jnp.arange(half, dtype=jnp.float32) * 2.0 / ROT_DIM)
    ang = pos.astype(jnp.float32)[:, None] * inv[None, :]
    shape = (ang.shape[0],) + (1,) * (x.ndim - 3) + (half,)
    cos = jnp.cos(ang).reshape(shape)
    sin = jnp.sin(ang).reshape(shape)
    xf = x.astype(jnp.float32)
    x1 = xf[..., :half]
    x2 = xf[..., half:ROT_DIM]
    out = jnp.concatenate([x1 * cos - x2 * sin, x1 * sin + x2 * cos, xf[..., ROT_DIM:]], axis=-1)
    return out.astype(x.dtype)


def masked_softmax(s, mask):
    s = jnp.where(mask, s, NEG)
    m = jnp.max(s, axis=-1, keepdims=True)
    e = jnp.where(mask, jnp.exp(s - m), 0.0)
    den = jnp.sum(e, axis=-1, keepdims=True)
    return e / jnp.where(den > 0, den, 1.0)


def gather_pages(pool, page_table):
    g = pool[page_table]
    return g.reshape((g.shape[0], g.shape[1] * g.shape[2]) + g.shape[3:])


def over_query_blocks(fn, n_tokens):
    starts = jnp.arange(n_tokens // QBLK, dtype=jnp.int32) * QBLK
    out = lax.map(fn, starts)
    out = jnp.moveaxis(out, 0, 1)
    return out.reshape((out.shape[0], n_tokens) + out.shape[3:])


def stick_breaking(q, k, v, q_pos, k_pos):
    z = jnp.einsum('bqhd,bkhd->bhqk', q, k).astype(jnp.float32) / math.sqrt(HEAD_DIM)
    mask = k_pos[None, :] < q_pos[:, None]
    log_keep = jnp.where(mask, jax.nn.log_sigmoid(-z), 0.0)
    later = lax.cumsum(log_keep, axis=3, reverse=True) - log_keep
    w = jnp.where(mask, jnp.exp(jax.nn.log_sigmoid(z) + later), 0.0)
    return jnp.einsum('bhqk,bkhd->bqhd', w.astype(v.dtype), v)


def diff_attention(q, k, v, q_pos, k_pos, lam, lam_init, subln_g):
    s = jnp.einsum('bqhcd,bkhcd->bhcqk', q, k).astype(jnp.float32) / math.sqrt(HEAD_DIM)
    p = masked_softmax(s, k_pos[None, :] <= q_pos[:, None])
    a = p[:, :, 0] - lam * p[:, :, 1]
    o = jnp.einsum('bhqk,bkhe->bqhe', a.astype(v.dtype), v)
    return rms_norm(o, subln_g) * (1.0 - lam_init)


def ab_project(h, w_in, pos):
    b, t, _ = h.shape
    z = h @ w_in
    qa, ka, va, qb, kb, vb = jnp.split(z, [SB_W, 2 * SB_W, 3 * SB_W, 3 * SB_W + DF_W, 3 * SB_W + 2 * DF_W], axis=-1)
    qa = qa.reshape(b, t, SB_HEADS, HEAD_DIM)
    ka = ka.reshape(b, t, SB_HEADS, HEAD_DIM)
    va = va.reshape(b, t, SB_HEADS, HEAD_DIM)
    qb = rope(qb.reshape(b, t, DIFF_HEADS, 2, HEAD_DIM), pos)
    kb = rope(kb.reshape(b, t, DIFF_HEADS, 2, HEAD_DIM), pos)
    k_rows = jnp.concatenate([ka, kb.reshape(b, t, 2 * DIFF_HEADS, HEAD_DIM)], axis=2)
    v_rows = jnp.concatenate([va, vb.reshape(b, t, 2 * DIFF_HEADS, HEAD_DIM)], axis=2)
    return qa, qb, jnp.stack([k_rows, v_rows], axis=2)


def ab_attend(qa, qb, q_pos, kv, k_pos, lam, lam_init, subln_g):
    b, tk = kv.shape[:2]
    tq = qa.shape[1]
    k = kv[:, :, 0]
    v = kv[:, :, 1]
    oa = stick_breaking(qa, k[:, :, :SB_HEADS], v[:, :, :SB_HEADS], q_pos, k_pos)
    kb = k[:, :, SB_HEADS:].reshape(b, tk, DIFF_HEADS, 2, HEAD_DIM)
    vb = v[:, :, SB_HEADS:].reshape(b, tk, DIFF_HEADS, 2 * HEAD_DIM)
    ob = diff_attention(qb, kb, vb, q_pos, k_pos, lam, lam_init, subln_g)
    return jnp.concatenate([oa.reshape(b, tq, SB_W), ob.reshape(b, tq, DF_W)], axis=-1)


def ab_prompt(h, pos, w_in, lam, lam_init, subln_g):
    qa, qb, kv = ab_project(h, w_in, pos)

    def block(start):
        qp = start + jnp.arange(QBLK, dtype=jnp.int32)
        return ab_attend(lax.dynamic_slice_in_dim(qa, start, QBLK, 1), lax.dynamic_slice_in_dim(qb, start, QBLK, 1),
                         qp, kv, pos, lam, lam_init, subln_g)
    return over_query_blocks(block, h.shape[1]), kv


def ab_sample(h, pos, cache, page_table, w_in, lam, lam_init, subln_g):
    qa, qb, kv_new = ab_project(h, w_in, pos)
    kv_all = jnp.concatenate([gather_pages(cache, page_table), kv_new], axis=1)
    k_pos = jnp.arange(kv_all.shape[1], dtype=jnp.int32)
    return ab_attend(qa, qb, pos, kv_all, k_pos, lam, lam_init, subln_g), kv_new


def nsa_project(h, w_in, pos):
    b, t, _ = h.shape
    z = h @ w_in
    qw = NSA_HEADS * HEAD_DIM
    kw = 6 * NSA_GROUPS * HEAD_DIM
    q = z[..., :qw].reshape(b, t, NSA_HEADS, HEAD_DIM)
    rows = z[..., qw:qw + kw].reshape(b, t, 6, NSA_GROUPS, HEAD_DIM)
    gates = jax.nn.sigmoid(z[..., qw + kw:]).reshape(b, t, NSA_HEADS, 3)
    k_slc = rope(rows[:, :, 2], pos)
    k_win = rope(rows[:, :, 4], pos)
    paged_rows = jnp.stack([rows[:, :, 0], rows[:, :, 1], k_slc, rows[:, :, 3]], axis=2)
    win_rows = jnp.stack([k_win, rows[:, :, 5]], axis=2)
    return q, rope(q, pos), gates, paged_rows, win_rows


def nsa_compress(rows, pe, phi):
    b, t, g, d = rows.shape
    blocks = rows.reshape(b, t // BLK, BLK, g, d) + pe[:, None, :]
    return jnp.einsum('bnlgd,lde->bnge', blocks, phi.reshape(BLK, d, d))


def nsa_blocks(rows, pe, phi):
    b, t = rows.shape[:2]
    ck = nsa_compress(rows[:, :, 0], pe[0], phi[0])
    cv = nsa_compress(rows[:, :, 1], pe[1], phi[1])
    kblk = rows[:, :, 2].reshape(b, t // BLK, BLK, NSA_GROUPS, HEAD_DIM).transpose(0, 3, 1, 2, 4)
    vblk = rows[:, :, 3].reshape(b, t // BLK, BLK, NSA_GROUPS, HEAD_DIM).transpose(0, 3, 1, 2, 4)
    return ck, cv, kblk, vblk


def nsa_core(q, q_rot, gates, t, ck, cv, kblk, vblk, kw, vw, w_pos):
    b, tq = q.shape[:2]
    hg = NSA_HEADS // NSA_GROUPS
    nb = ck.shape[1]
    scale = 1.0 / math.sqrt(HEAD_DIM)
    qg = q.reshape(b, tq, NSA_GROUPS, hg, HEAD_DIM)
    qr = q_rot.reshape(b, tq, NSA_GROUPS, hg, HEAD_DIM)
    blk = jnp.arange(nb, dtype=jnp.int32)
    done = (blk[None, :] + 1) * BLK - 1 <= t[:, None]
    s = jnp.einsum('bqghd,bngd->bqghn', qg, ck).astype(jnp.float32) * scale
    p_cmp = masked_softmax(s, done[None, :, None, None, :])
    o_cmp = jnp.einsum('bqghn,bngd->bqghd', p_cmp.astype(cv.dtype), cv)
    cur = (t // BLK)[:, None]
    forced = (blk[None, :] == cur) | (blk[None, :] == 0)
    future = blk[None, :] > cur
    imp = jnp.sum(p_cmp, axis=3)
    imp = jnp.where(forced[None, :, None, :], FORCE, jnp.where(future[None, :, None, :], -1.0, imp))
    _, idx = lax.top_k(imp, min(TOPK, nb))
    bi = jnp.arange(b)[:, None, None, None]
    gi = jnp.arange(NSA_GROUPS)[None, None, :, None]
    ks = kblk[bi, gi, idx].reshape(b, tq, NSA_GROUPS, -1, HEAD_DIM)
    vs = vblk[bi, gi, idx].reshape(b, tq, NSA_GROUPS, -1, HEAD_DIM)
    kpos = (idx[..., None] * BLK + jnp.arange(BLK, dtype=jnp.int32)).reshape(b, tq, NSA_GROUPS, -1)
    s = jnp.einsum('bqghd,bqgkd->bqghk', qr, ks).astype(jnp.float32) * scale
    p = masked_softmax(s, (kpos <= t[None, :, None, None])[:, :, :, None, :])
    o_slc = jnp.einsum('bqghk,bqgkd->bqghd', p.astype(vs.dtype), vs)
    wmask = (w_pos[None, :] <= t[:, None]) & (w_pos[None, :] > t[:, None] - WINDOW) & (w_pos[None, :] >= 0)
    s = jnp.einsum('bqghd,bkgd->bqghk', qr, kw).astype(jnp.float32) * scale
    p = masked_softmax(s, wmask[None, :, None, None, :])
    o_win = jnp.einsum('bqghk,bkgd->bqghd', p.astype(vw.dtype), vw)
    gg = gates.reshape(b, tq, NSA_GROUPS, hg, 3)
    o = gg[..., 0:1] * o_cmp + gg[..., 1:2] * o_slc + gg[..., 2:3] * o_win
    return o.reshape(b, tq, NSA_HEADS * HEAD_DIM)


def nsa_prompt(h, pos, w_in, pe, phi):
    q, q_rot, gates, prow, wrow = nsa_project(h, w_in, pos)
    ck, cv, kblk, vblk = nsa_blocks(prow, pe, phi)
    wpad = jnp.pad(wrow, ((0, 0), (WINDOW, 0), (0, 0), (0, 0), (0, 0)))

    def block(start):
        t = start + jnp.arange(QBLK, dtype=jnp.int32)
        wb = lax.dynamic_slice_in_dim(wpad, start, WINDOW + QBLK, 1)
        w_pos = start - WINDOW + jnp.arange(WINDOW + QBLK, dtype=jnp.int32)
        return nsa_core(lax.dynamic_slice_in_dim(q, start, QBLK, 1), lax.dynamic_slice_in_dim(q_rot, start, QBLK, 1),
                        lax.dynamic_slice_in_dim(gates, start, QBLK, 1), t, ck, cv, kblk, vblk,
                        wb[:, :, 0], wb[:, :, 1], w_pos)
    out = over_query_blocks(block, h.shape[1])
    n_keep = min(WINDOW, h.shape[1])
    return out, prow, wrow[:, h.shape[1] - n_keep:]


def nsa_sample(h, pos, cache, win_state, page_table, w_in, pe, phi):
    q, q_rot, gates, prow, wrow = nsa_project(h, w_in, pos)
    rows = jnp.concatenate([gather_pages(cache, page_table), prow], axis=1)
    t_all = rows.shape[1]
    t_pad = -(-t_all // BLK) * BLK
    rows = jnp.pad(rows, ((0, 0), (0, t_pad - t_all), (0, 0), (0, 0), (0, 0)))
    ck, cv, kblk, vblk = nsa_blocks(rows, pe, phi)
    n_buf = win_state.shape[1]
    win_all = jnp.concatenate([win_state, wrow], axis=1)
    w_pos = pos[0] - n_buf + jnp.arange(win_all.shape[1], dtype=jnp.int32)
    out = nsa_core(q, q_rot, gates, pos, ck, cv, kblk, vblk, win_all[:, :, 0], win_all[:, :, 1], w_pos)
    return out, prow, win_all[:, win_all.shape[1] - n_buf:]


def mem_kv(mem, g, w_kv):
    b, m, _ = mem.shape
    return (rms_norm(mem, g) @ w_kv).reshape(b, m, 2, MEM_HEADS, MEM_HEAD_DIM)


def mem_attend(h, kv, w_q, w_o):
    b, t, _ = h.shape
    q = (h @ w_q).reshape(b, t, MEM_HEADS, MEM_HEAD_DIM)
    s = jnp.einsum('bqhd,bmhd->bhqm', q, kv[:, :, 0]).astype(jnp.float32) / math.sqrt(MEM_HEAD_DIM)
    p = jax.nn.softmax(s, axis=-1)
    o = jnp.einsum('bhqm,bmhd->bqhd', p.astype(kv.dtype), kv[:, :, 1])
    return o.reshape(b, t, MEM_W) @ w_o


def setup_inputs(seed: int = 0) -> dict:
    key = jax.random.key(seed)
    keys = jax.random.split(key, 40)

    def nrm(i, shape, scale):
        return scale * jax.random.normal(keys[i], shape, jnp.float32)

    n_pages = PAST_LEN // PAGE_SIZE
    n_used = DEC_BATCH * n_pages
    n_pool = n_used + max(1, n_used // 4)
    w_buf = min(WINDOW, PAST_LEN)
    page_table = jax.random.permutation(keys[0], n_pool)[:n_used].reshape(DEC_BATCH, n_pages).astype(jnp.int32)
    kv_shape = (n_pool, PAGE_SIZE, 2, AB_KV_HEADS, HEAD_DIM)
    nsa_shape = (n_pool, PAGE_SIZE, 4, NSA_GROUPS, HEAD_DIM)
    win_shape = (DEC_BATCH, w_buf, 2, NSA_GROUPS, HEAD_DIM)
    mem_shape = (DEC_BATCH, MEM_LEN, 2, MEM_HEADS, MEM_HEAD_DIM)
    return {
        'x_prompt': nrm(1, (BATCH, SEQ, D_MODEL), 1.0),
        'x_sample': nrm(2, (DEC_BATCH, DEC_SEQ, D_MODEL), 1.0),
        'mem_prompt': nrm(3, (BATCH, MEM_LEN, D_MODEL), 1.0),
        'page_table': page_table,
        'cache_kv_l0': nrm(4, kv_shape, 1.0),
        'cache_mem_l0': nrm(5, mem_shape, 1.0),
        'cache_nsa_l1': nrm(6, nsa_shape, 1.0),
        'state_win_l1': nrm(7, win_shape, 1.0),
        'cache_mem_l1': nrm(8, mem_shape, 1.0),
        'cache_kv_l2': nrm(9, kv_shape, 1.0),
        'cache_mem_l2': nrm(10, mem_shape, 1.0),
        'cache_nsa_l3': nrm(11, nsa_shape, 1.0),
        'state_win_l3': nrm(12, win_shape, 1.0),
        'cache_mem_l3': nrm(13, mem_shape, 1.0),
        'norm_g': 1.0 + nrm(14, (DEPTH, 8, D_MODEL), 0.1),
        'mem_norm_g': 1.0 + nrm(15, (DEPTH, D_MODEL), 0.1),
        'ffn_w_in': nrm(16, (DEPTH, 2, D_MODEL, 2 * D_FF), D_MODEL ** -0.5),
        'ffn_w_out': nrm(17, (DEPTH, 2, D_FF, D_MODEL), D_FF ** -0.5),
        'w_q_mem': nrm(18, (DEPTH, D_MODEL, MEM_W), D_MODEL ** -0.5),
        'w_kv_mem': nrm(19, (DEPTH, D_MODEL, 2 * MEM_W), D_MODEL ** -0.5),
        'w_o_mem': nrm(20, (DEPTH, MEM_W, D_MODEL), MEM_W ** -0.5),
        'ab_w_in': nrm(21, (N_EVEN, D_MODEL, AB_IN), D_MODEL ** -0.5),
        'ab_w_out': nrm(22, (N_EVEN, AB_OUT, D_MODEL), AB_OUT ** -0.5),
        'diff_lambda': nrm(23, (N_EVEN, 4, HEAD_DIM), 0.1),
        'diff_subln_g': 1.0 + nrm(24, (N_EVEN, 2 * HEAD_DIM), 0.1),
        'nsa_w_in': nrm(25, (N_ODD, D_MODEL, NSA_IN), D_MODEL ** -0.5),
        'nsa_w_out': nrm(26, (N_ODD, NSA_HEADS * HEAD_DIM, D_MODEL), (NSA_HEADS * HEAD_DIM) ** -0.5),
        'nsa_cmp_pe': nrm(27, (N_ODD, 2, BLK, HEAD_DIM), 0.1),
        'nsa_cmp_phi': nrm(28, (N_ODD, 2, BLK * HEAD_DIM, HEAD_DIM), (BLK * HEAD_DIM) ** -0.5),
    }


def reference(x_prompt, x_sample, mem_prompt, page_table,
              cache_kv_l0, cache_mem_l0, cache_nsa_l1, state_win_l1, cache_mem_l1,
              cache_kv_l2, cache_mem_l2, cache_nsa_l3, state_win_l3, cache_mem_l3,
              norm_g, mem_norm_g, ffn_w_in, ffn_w_out, w_q_mem, w_kv_mem, w_o_mem,
              ab_w_in, ab_w_out, diff_lambda, diff_subln_g,
              nsa_w_in, nsa_w_out, nsa_cmp_pe, nsa_cmp_phi):
    pos_p = jnp.arange(x_prompt.shape[1], dtype=jnp.int32)
    past = page_table.shape[1] * cache_kv_l0.shape[1]
    pos_s = past + jnp.arange(x_sample.shape[1], dtype=jnp.int32)
    kv_caches = (cache_kv_l0, cache_kv_l2)
    nsa_caches = (cache_nsa_l1, cache_nsa_l3)
    win_states = (state_win_l1, state_win_l3)
    mem_caches = (cache_mem_l0, cache_mem_l1, cache_mem_l2, cache_mem_l3)
    kv_p, kv_s, nsa_p, nsa_s, win_p, win_s, mem_p = [], [], [], [], [], [], []
    xp, xs = x_prompt, x_sample
    for layer in range(DEPTH):
        g = norm_g[layer]
        xp = ffn_half(xp, g[0], g[1], ffn_w_in[layer, 0], ffn_w_out[layer, 0])
        xs = ffn_half(xs, g[0], g[1], ffn_w_in[layer, 0], ffn_w_out[layer, 0])
        hp = rms_norm(xp, g[2])
        hs = rms_norm(xs, g[2])
        if layer % 2 == 0:
            i = layer // 2
            lam_init = 0.8 - 0.6 * math.exp(-0.3 * layer)
            lv = diff_lambda[i].astype(jnp.float32)
            lam = jnp.exp(jnp.sum(lv[0] * lv[1])) - jnp.exp(jnp.sum(lv[2] * lv[3])) + lam_init
            mp, rows_p = ab_prompt(hp, pos_p, ab_w_in[i], lam, lam_init, diff_subln_g[i])
            ms, rows_s = ab_sample(hs, pos_s, kv_caches[i], page_table, ab_w_in[i], lam, lam_init, diff_subln_g[i])
            mp = mp @ ab_w_out[i]
            ms = ms @ ab_w_out[i]
            kv_p.append(rows_p)
            kv_s.append(rows_s)
        else:
            i = layer // 2
            mp, rows_p, wp = nsa_prompt(hp, pos_p, nsa_w_in[i], nsa_cmp_pe[i], nsa_cmp_phi[i])
            ms, rows_s, ws = nsa_sample(hs, pos_s, nsa_caches[i], win_states[i], page_table,
                                        nsa_w_in[i], nsa_cmp_pe[i], nsa_cmp_phi[i])
            mp = mp @ nsa_w_out[i]
            ms = ms @ nsa_w_out[i]
            nsa_p.append(rows_p)
            nsa_s.append(rows_s)
            win_p.append(wp)
            win_s.append(ws)
        xp = xp + rms_norm(mp, g[3])
        xs = xs + rms_norm(ms, g[3])
        mkv = mem_kv(mem_prompt, mem_norm_g[layer], w_kv_mem[layer])
        xp = xp + rms_norm(mem_attend(rms_norm(xp, g[4]), mkv, w_q_mem[layer], w_o_mem[layer]), g[5])
        xs = xs + rms_norm(mem_attend(rms_norm(xs, g[4]), mem_caches[layer], w_q_mem[layer], w_o_mem[layer]), g[5])
        mem_p.append(mkv)
        xp = ffn_half(xp, g[6], g[7], ffn_w_in[layer, 1], ffn_w_out[layer, 1])
        xs = ffn_half(xs, g[6], g[7], ffn_w_in[layer, 1], ffn_w_out[layer, 1])
    return (xp, xs,
            kv_p[0], kv_s[0], mem_p[0],
            nsa_p[0], nsa_s[0], win_p[0], win_s[0], mem_p[1],
            kv_p[1], kv_s[1], mem_p[2],
            nsa_p[1], nsa_s[1], win_p[1], win_s[1], mem_p[3])
```

```python
import functools
import math

import jax
import jax.numpy as jnp
from jax import lax
from jax.experimental import pallas as pl
from jax.experimental.pallas import tpu as pltpu

F32 = jnp.float32
BF16 = jnp.bfloat16

HEAD_DIM = 64
ROT_DIM = HEAD_DIM // 4
ROPE_THETA = 500000.0
BLK = 64
TOPK = 8
WINDOW = 256
EPS = 1e-6
NEG = -1e30
FORCE = 1e4
LANES = 128
QK_SCALE = 1.0 / math.sqrt(HEAD_DIM)

VMEM_LIMIT = 56 * 1024 * 1024


def _cp(*sem):
    return pltpu.CompilerParams(dimension_semantics=sem, vmem_limit_bytes=VMEM_LIMIT)


def _rms(x, g):
    return x * lax.rsqrt(jnp.mean(x * x, axis=-1, keepdims=True) + EPS) * g


def _dot(a, b):
    return jnp.dot(a, b, preferred_element_type=F32)


def _dot_nt(a, b):
    return lax.dot_general(a, b, (((1,), (1,)), ((), ())), preferred_element_type=F32)


def _softplus(z):
    return jnp.maximum(z, 0.0) + jnp.log1p(jnp.exp(-jnp.abs(z)))


def _split_bf16(x):
    hi = x.astype(BF16)
    lo = (x - hi.astype(F32)).astype(BF16)
    return hi, lo


def _tile(n, pref):
    t = min(n, pref)
    while n % t:
        t //= 2
    return t


def _rope_tables(pos):
    half = ROT_DIM // 2
    inv = jnp.power(ROPE_THETA, -jnp.arange(half, dtype=F32) * 2.0 / ROT_DIM)
    ang = pos.astype(F32)[:, None] * inv[None, :]
    cos = jnp.cos(ang)
    sin = jnp.sin(ang)
    p = pos.shape[0]
    ones = jnp.ones((p, HEAD_DIM - ROT_DIM), F32)
    zeros = jnp.zeros((p, HEAD_DIM - ROT_DIM), F32)
    zh = jnp.zeros((p, half), F32)
    c = jnp.concatenate([cos, cos, ones], axis=1)
    s1 = jnp.concatenate([-sin, zh, zeros], axis=1)
    s2 = jnp.concatenate([zh, sin, zeros], axis=1)
    rep = LANES // HEAD_DIM
    return jnp.tile(c, (1, rep)), jnp.tile(s1, (1, rep)), jnp.tile(s2, (1, rep))


def _rope_cols(z, c, s1, s2):
    outs = []
    for i in range(z.shape[1] // LANES):
        xb = z[:, i * LANES:(i + 1) * LANES]
        outs.append(xb * c + pltpu.roll(xb, LANES - ROT_DIM // 2, 1) * s1 + pltpu.roll(xb, ROT_DIM // 2, 1) * s2)
    return outs[0] if len(outs) == 1 else jnp.concatenate(outs, axis=1)


def _ffn_kernel(*refs, nj, pre, ipre, ipost, imem):
    if pre:
        x_ref, om_ref, wom_ref, g_ref, wg_ref, wu_ref, wo_ref, o_ref, h_scr, acc_scr, x_scr = refs
    else:
        x_ref, g_ref, wg_ref, wu_ref, wo_ref, o_ref, h_scr, acc_scr, x_scr = refs
    j = pl.program_id(1)

    @pl.when(j == 0)
    def _():
        x = x_ref[...]
        if pre:
            y = _dot(om_ref[...].astype(BF16), wom_ref[...])
            x = x + _rms(y, g_ref[imem:imem + 1, :])
        x_scr[...] = x
        h_scr[...] = _rms(x, g_ref[ipre:ipre + 1, :]).astype(BF16)
        acc_scr[...] = jnp.zeros_like(acc_scr)

    h = h_scr[...]
    gate = _dot(h, wg_ref[...])
    up = _dot(h, wu_ref[...])
    act = (gate * jax.nn.sigmoid(gate)) * up
    acc_scr[...] += _dot(act.astype(BF16), wo_ref[...])

    @pl.when(j == nj - 1)
    def _():
        o_ref[...] = x_scr[...] + 0.5 * _rms(acc_scr[...], g_ref[ipost:ipost + 1, :])


def _ffn_half(x, g, w_in, w_out, ipre, ipost, om=None, w_om=None, imem=None):
    n, d = x.shape
    dff = w_out.shape[0]
    tm = _tile(n, 512)
    tf = dff // 2 if (dff // 2) % LANES == 0 else dff
    nj = dff // tf
    pre = om is not None
    kern = functools.partial(_ffn_kernel, nj=nj, pre=pre, ipre=ipre, ipost=ipost, imem=imem)
    row = lambda i, j: (i, 0)
    in_specs = [pl.BlockSpec((tm, d), row)]
    args = [x]
    if pre:
        in_specs += [pl.BlockSpec((tm, om.shape[1]), row), pl.BlockSpec(w_om.shape, lambda i, j: (0, 0))]
        args += [om, w_om]
    in_specs += [
        pl.BlockSpec(g.shape, lambda i, j: (0, 0)),
        pl.BlockSpec((d, tf), lambda i, j: (0, j)),
        pl.BlockSpec((d, tf), lambda i, j: (0, nj + j)),
        pl.BlockSpec((tf, d), lambda i, j: (j, 0)),
    ]
    args += [g, w_in, w_in, w_out]
    return pl.pallas_call(
        kern,
        grid=(n // tm, nj),
        in_specs=in_specs,
        out_specs=pl.BlockSpec((tm, d), row),
        out_shape=jax.ShapeDtypeStruct((n, d), F32),
        scratch_shapes=[pltpu.VMEM((tm, d), BF16), pltpu.VMEM((tm, d), F32), pltpu.VMEM((tm, d), F32)],
        compiler_params=_cp("parallel", "arbitrary"),
        name="ffn_half",
    )(*args)


def _ab_proj_kernel(x_ref, g_ref, w_ref, c_ref, s1_ref, s2_ref, qz_ref, kv_ref, *, sbw):
    h = _rms(x_ref[...], g_ref[2:3, :]).astype(BF16)
    c, s1, s2 = c_ref[...], s1_ref[...], s2_ref[...]

    def proj(i):
        return _dot(h, w_ref[:, i * sbw:(i + 1) * sbw])

    qz_ref[:, 0:sbw] = proj(0)
    kv_ref[:, 0:sbw] = proj(1)
    kv_ref[:, 2 * sbw:3 * sbw] = proj(2)
    qz_ref[:, sbw:2 * sbw] = _rope_cols(proj(3), c, s1, s2)
    kv_ref[:, sbw:2 * sbw] = _rope_cols(proj(4), c, s1, s2)
    kv_ref[:, 3 * sbw:4 * sbw] = proj(5)


def _ab_proj(x, g, w_in, tabs, tab_map, tm):
    n, d = x.shape
    sbw = w_in.shape[1] // 6
    row = lambda i: (i, 0)
    tspec = pl.BlockSpec((tm, LANES), tab_map)
    return pl.pallas_call(
        functools.partial(_ab_proj_kernel, sbw=sbw),
        grid=(n // tm,),
        in_specs=[pl.BlockSpec((tm, d), row), pl.BlockSpec(g.shape, lambda i: (0, 0)),
                  pl.BlockSpec(w_in.shape, lambda i: (0, 0)), tspec, tspec, tspec],
        out_specs=[pl.BlockSpec((tm, 2 * sbw), row), pl.BlockSpec((tm, 4 * sbw), row)],
        out_shape=[jax.ShapeDtypeStruct((n, 2 * sbw), F32), jax.ShapeDtypeStruct((n, 4 * sbw), F32)],
        compiler_params=_cp("parallel"),
        name="ab_proj",
    )(x, g, w_in, *tabs)


def _diff_lambda(lam_ref, lam_init):
    lv = lam_ref[...]
    a = jnp.sum(lv[0:1, :] * lv[1:2, :], axis=1, keepdims=True)
    b = jnp.sum(lv[2:3, :] * lv[3:4, :], axis=1, keepdims=True)
    return jnp.exp(a) - jnp.exp(b) + lam_init


def _ab_prompt_kernel(lam_ref, sg_ref, uu_ref, q_ref, k_ref, v_ref, o_ref, *, tq, n_sb, lam_init):
    u = pl.program_id(1)
    qi = pl.program_id(2)
    q = q_ref[0]
    lane_half = lax.broadcasted_iota(jnp.int32, (tq, LANES), 1) // HEAD_DIM
    row = lax.broadcasted_iota(jnp.int32, (tq, tq), 0)
    col = lax.broadcasted_iota(jnp.int32, (tq, tq), 1)

    def load_kv(kb):
        start = pl.multiple_of(kb * tq, tq)
        return k_ref[0, pl.ds(start, tq), :].astype(BF16), v_ref[0, pl.ds(start, tq), :].astype(BF16)

    @pl.when(u < n_sb)
    def _():
        uu = uu_ref[...]
        out = jnp.zeros((tq, LANES), F32)
        for hh in range(2):
            qp = jnp.where(lane_half == hh, q, 0.0).astype(BF16)

            def body(i, carry):
                run, acc = carry
                kb = qi - i
                k, v = load_kv(kb)
                z = _dot_nt(qp, k) * QK_SCALE
                mask = (kb * tq + col) < (qi * tq + row)
                sp = _softplus(z)
                lk = jnp.where(mask, -sp, 0.0)
                hi, lo = _split_bf16(lk)
                later = _dot(jnp.concatenate([hi, lo], axis=1), uu) + run
                w = jnp.where(mask, jnp.exp(z - sp + later), 0.0)
                acc = acc + _dot(w.astype(BF16), v)
                run = run + jnp.sum(lk, axis=1, keepdims=True)
                return run, acc

            _, acc = lax.fori_loop(0, qi + 1, body, (jnp.zeros((tq, 1), F32), jnp.zeros((tq, LANES), F32)))
            out = out + jnp.where(lane_half == hh, acc, 0.0)
        o_ref[0] = out

    @pl.when(u >= n_sb)
    def _():
        lam = _diff_lambda(lam_ref, lam_init)
        parts = []
        for c in range(2):
            qp = jnp.where(lane_half == c, q, 0.0).astype(BF16)

            def body(kb, carry):
                m, l, acc = carry
                k, v = load_kv(kb)
                s = _dot_nt(qp, k) * QK_SCALE
                mask = (kb * tq + col) <= (qi * tq + row)
                s = jnp.where(mask, s, NEG)
                m_new = jnp.maximum(m, jnp.max(s, axis=1, keepdims=True))
                alpha = jnp.exp(m - m_new)
                e = jnp.where(mask, jnp.exp(s - m_new), 0.0)
                l = alpha * l + jnp.sum(e, axis=1, keepdims=True)
                acc = alpha * acc + _dot(e.astype(BF16), v)
                return m_new, l, acc

            m, l, acc = lax.fori_loop(
                0, qi + 1, body,
                (jnp.full((tq, 1), NEG, F32), jnp.zeros((tq, 1), F32), jnp.zeros((tq, LANES), F32)))
            parts.append(acc / jnp.where(l > 0, l, 1.0))
        o = parts[0] - lam * parts[1]
        o_ref[0] = _rms(o, sg_ref[...]) * (1.0 - lam_init)


def _cumsum_matrix(tk):
    j = jnp.arange(2 * tk)[:, None] % tk
    s = jnp.arange(tk)[None, :]
    return (j > s).astype(BF16)


def _ab_prompt_attn(qz, kv, lam, subln_g, lam_init):
    b, t, w = qz.shape
    nu = w // LANES
    tq = _tile(t, 256)
    kern = functools.partial(_ab_prompt_kernel, tq=tq, n_sb=nu // 2, lam_init=lam_init)
    return pl.pallas_call(
        kern,
        grid=(b, nu, t // tq),
        in_specs=[
            pl.BlockSpec(lam.shape, lambda b_, u, i: (0, 0)),
            pl.BlockSpec((1, LANES), lambda b_, u, i: (0, 0)),
            pl.BlockSpec((2 * tq, tq), lambda b_, u, i: (0, 0)),
            pl.BlockSpec((1, tq, LANES), lambda b_, u, i: (b_, i, u)),
            pl.BlockSpec((1, t, LANES), lambda b_, u, i: (b_, 0, u)),
            pl.BlockSpec((1, t, LANES), lambda b_, u, i: (b_, 0, nu + u)),
        ],
        out_specs=pl.BlockSpec((1, tq, LANES), lambda b_, u, i: (b_, i, u)),
        out_shape=jax.ShapeDtypeStruct((b, t, w), F32),
        compiler_params=_cp("parallel", "parallel", "arbitrary"),
        name="ab_prompt_attn",
    )(lam, subln_g.reshape(1, LANES), _cumsum_matrix(tq), qz, kv, kv)


def _ab_sample_kernel(pt_ref, lam_ref, sg_ref, uu_ref, q_ref, kvn_ref, *rest, npg, page, nq, past, lam_init):
    pages = rest[:npg]
    o_ref = rest[npg]
    qbd_f, qbd, newpage, o_acc, run_s, m_s, l_s = rest[npg + 1:]
    j = pl.program_id(1)
    nj = pl.num_programs(1)
    nu = q_ref.shape[2] // LANES
    rows = nu * 2 * nq
    hr = rows // 2
    kw = nu * LANES

    @pl.when(j == 0)
    def _():
        q = q_ref[0]
        lane_half = lax.broadcasted_iota(jnp.int32, (nq, LANES), 1) // HEAD_DIM
        qbd_f[...] = jnp.zeros_like(qbd_f)
        for u in range(nu):
            qu = q[:, u * LANES:(u + 1) * LANES]
            piece = jnp.concatenate([jnp.where(lane_half == 0, qu, 0.0), jnp.where(lane_half == 1, qu, 0.0)], axis=0)
            qbd_f[u * 2 * nq:(u + 1) * 2 * nq, u * LANES:(u + 1) * LANES] = piece
        qbd[...] = qbd_f[...].astype(BF16)
        newpage[...] = jnp.zeros_like(newpage)
        newpage[0:nq, :] = kvn_ref[0]
        o_acc[...] = jnp.zeros_like(o_acc)
        run_s[...] = jnp.zeros_like(run_s)
        m_s[...] = jnp.full_like(m_s, NEG)
        l_s[...] = jnp.zeros_like(l_s)

    lane = lax.broadcasted_iota(jnp.int32, (hr, page), 1)
    qpos = past + lax.broadcasted_iota(jnp.int32, (hr, page), 0) % nq
    uu = uu_ref[...]

    def process(kvp, page_start):
        k = kvp[:, 0:kw].astype(BF16)
        v = kvp[:, kw:2 * kw].astype(BF16)
        s = _dot_nt(qbd[...], k) * QK_SCALE
        kpos = page_start + lane
        z = s[0:hr]
        mask = kpos < qpos
        sp = _softplus(z)
        lk = jnp.where(mask, -sp, 0.0)
        hi, lo = _split_bf16(lk)
        later = _dot(jnp.concatenate([hi, lo], axis=1), uu) + run_s[...]
        w = jnp.where(mask, jnp.exp(z - sp + later), 0.0)
        run_s[...] += jnp.sum(lk, axis=1, keepdims=True)
        sd = s[hr:rows]
        maskd = kpos <= qpos
        sd = jnp.where(maskd, sd, NEG)
        m_old = m_s[...]
        m_new = jnp.maximum(m_old, jnp.max(sd, axis=1, keepdims=True))
        alpha = jnp.exp(m_old - m_new)
        e = jnp.where(maskd, jnp.exp(sd - m_new), 0.0)
        l_s[...] = alpha * l_s[...] + jnp.sum(e, axis=1, keepdims=True)
        m_s[...] = m_new
        pv = _dot(jnp.concatenate([w, e], axis=0).astype(BF16), v)
        o_acc[0:hr, :] += pv[0:hr]
        o_acc[hr:rows, :] = alpha * o_acc[hr:rows, :] + pv[hr:rows]

    @pl.when(j == 0)
    def _():
        process(newpage[...], past)

    for p in range(npg):
        pidx = (nj * npg - 1) - (j * npg + p)
        process(pages[p][0], pidx * page)

    @pl.when(j == nj - 1)
    def _():
        lam = _diff_lambda(lam_ref, lam_init)
        lane_half = lax.broadcasted_iota(jnp.int32, (nq, LANES), 1) // HEAD_DIM
        outs = []
        for u in range(nu):
            blk = o_acc[u * 2 * nq:(u + 1) * 2 * nq, u * LANES:(u + 1) * LANES]
            if u < nu // 2:
                outs.append(jnp.where(lane_half == 0, blk[0:nq], blk[nq:2 * nq]))
            else:
                r0 = u * 2 * nq - hr
                l = l_s[r0:r0 + 2 * nq, :]
                l = jnp.where(l > 0, l, 1.0)
                o = blk[0:nq] / l[0:nq] - lam * (blk[nq:2 * nq] / l[nq:2 * nq])
                outs.append(_rms(o, sg_ref[...]) * (1.0 - lam_init))
        o_ref[0] = jnp.concatenate(outs, axis=1)


def _ab_sample_attn(qz, kv_new, cache, page_table, lam, subln_g, lam_init, npg):
    db, nq, w = qz.shape
    n_pages = page_table.shape[1]
    page = cache.shape[1]
    kvw = kv_new.shape[2]
    nu = w // LANES
    rows = nu * 2 * nq
    past = n_pages * page
    nj = n_pages // npg
    cache2 = cache.reshape(cache.shape[0], page, kvw)
    kern = functools.partial(_ab_sample_kernel, npg=npg, page=page, nq=nq, past=past, lam_init=lam_init)

    def page_spec(p):
        return pl.BlockSpec((1, page, kvw), lambda b, j, pt: (pt[b, n_pages - 1 - (j * npg + p)], 0, 0))

    grid_spec = pltpu.PrefetchScalarGridSpec(
        num_scalar_prefetch=1,
        grid=(db, nj),
        in_specs=[
            pl.BlockSpec(lam.shape, lambda b, j, pt: (0, 0)),
            pl.BlockSpec((1, LANES), lambda b, j, pt: (0, 0)),
            pl.BlockSpec((2 * page, page), lambda b, j, pt: (0, 0)),
            pl.BlockSpec((1, nq, w), lambda b, j, pt: (b, 0, 0)),
            pl.BlockSpec((1, nq, kvw), lambda b, j, pt: (b, 0, 0)),
        ] + [page_spec(p) for p in range(npg)],
        out_specs=pl.BlockSpec((1, nq, w), lambda b, j, pt: (b, 0, 0)),
        scratch_shapes=[
            pltpu.VMEM((rows, w), F32), pltpu.VMEM((rows, w), BF16), pltpu.VMEM((page, kvw), F32),
            pltpu.VMEM((rows, w), F32), pltpu.VMEM((rows // 2, 1), F32),
            pltpu.VMEM((rows // 2, 1), F32), pltpu.VMEM((rows // 2, 1), F32),
        ],
    )
    return pl.pallas_call(
        kern,
        grid_spec=grid_spec,
        out_shape=jax.ShapeDtypeStruct((db, nq, w), F32),
        compiler_params=_cp("parallel", "arbitrary"),
        name="ab_sample_attn",
    )(page_table, lam, subln_g.reshape(1, LANES), _cumsum_matrix(page), qz, kv_new, *([cache2] * npg))


def _mix_out_kernel(m_ref, x_ref, g_ref, wo_ref, wq_ref, x1_ref, qm_ref):
    y = _dot(m_ref[...].astype(BF16), wo_ref[...])
    x1 = x_ref[...] + _rms(y, g_ref[3:4, :])
    x1_ref[...] = x1
    qm_ref[...] = _dot(_rms(x1, g_ref[4:5, :]).astype(BF16), wq_ref[...])


def _mix_out(m, x, g, w_out, w_q):
    n, d = x.shape
    tm = _tile(n, 512)
    row = lambda i: (i, 0)
    full = lambda i: (0, 0)
    return pl.pallas_call(
        _mix_out_kernel,
        grid=(n // tm,),
        in_specs=[pl.BlockSpec((tm, m.shape[1]), row), pl.BlockSpec((tm, d), row), pl.BlockSpec(g.shape, full),
                  pl.BlockSpec(w_out.shape, full), pl.BlockSpec(w_q.shape, full)],
        out_specs=[pl.BlockSpec((tm, d), row), pl.BlockSpec((tm, w_q.shape[1]), row)],
        out_shape=[jax.ShapeDtypeStruct((n, d), F32), jax.ShapeDtypeStruct((n, w_q.shape[1]), F32)],
        compiler_params=_cp("parallel"),
        name="mix_out",
    )(m, x, g, w_out, w_q)


def _mem_attn_kernel(q_ref, kv_ref, o_ref, *, nh, hd):
    q = q_ref[0]
    kv = kv_ref[0]
    scale = 1.0 / math.sqrt(hd)
    outs = []
    for h in range(nh):
        qh = q[:, h * hd:(h + 1) * hd].astype(BF16)
        kh = kv[:, h * hd:(h + 1) * hd].astype(BF16)
        vh = kv[:, (nh + h) * hd:(nh + h + 1) * hd].astype(BF16)
        s = _dot_nt(qh, kh) * scale
        m = jnp.max(s, axis=1, keepdims=True)
        e = jnp.exp(s - m)
        p = e / jnp.sum(e, axis=1, keepdims=True)
        outs.append(_dot(p.astype(BF16), vh))
    o_ref[0] = jnp.concatenate(outs, axis=1)


def _mem_attn(q, kv, nh):
    b, t, w = q.shape
    mlen = kv.shape[1]
    tq = _tile(t, 512)
    return pl.pallas_call(
        functools.partial(_mem_attn_kernel, nh=nh, hd=w // nh),
        grid=(b, t // tq),
        in_specs=[pl.BlockSpec((1, tq, w), lambda b_, i: (b_, i, 0)),
                  pl.BlockSpec((1, mlen, 2 * w), lambda b_, i: (b_, 0, 0))],
        out_specs=pl.BlockSpec((1, tq, w), lambda b_, i: (b_, i, 0)),
        out_shape=jax.ShapeDtypeStruct((b, t, w), F32),
        compiler_params=_cp("parallel", "arbitrary"),
        name="mem_attn",
    )(q, kv)


def _mem_kv_kernel(x_ref, g_ref, w_ref, o_ref):
    o_ref[0] = _dot(_rms(x_ref[...], g_ref[0]).astype(BF16), w_ref[0])


def _mem_kv(mem, g, w_kv):
    n, d = mem.shape
    nl, _, wo = w_kv.shape
    tm = _tile(n, 512)
    return pl.pallas_call(
        _mem_kv_kernel,
        grid=(nl, n // tm),
        in_specs=[pl.BlockSpec((tm, d), lambda l, i: (i, 0)), pl.BlockSpec((1, 1, d), lambda l, i: (l, 0, 0)),
                  pl.BlockSpec((1, d, wo), lambda l, i: (l, 0, 0))],
        out_specs=pl.BlockSpec((1, tm, wo), lambda l, i: (l, i, 0)),
        out_shape=jax.ShapeDtypeStruct((nl, n, wo), F32),
        compiler_params=_cp("parallel", "parallel"),
        name="mem_kv",
    )(mem, g, w_kv)


def _nsa_proj_kernel(x_ref, g_ref, w_ref, c_ref, s1_ref, s2_ref, qq_ref, prow_ref, wrow_ref, gate_ref, *, qw):
    h = _rms(x_ref[...], g_ref[2:3, :]).astype(BF16)
    c, s1, s2 = c_ref[...], s1_ref[...], s2_ref[...]
    hw = qw // 2
    for i in range(2):
        z = _dot(h, w_ref[:, i * hw:(i + 1) * hw])
        qq_ref[:, i * hw:(i + 1) * hw] = z
        qq_ref[:, qw + i * hw:qw + (i + 1) * hw] = _rope_cols(z, c, s1, s2)
    zr = _dot(h, w_ref[:, qw:qw + 6 * LANES])
    prow_ref[:, 0:2 * LANES] = zr[:, 0:2 * LANES]
    prow_ref[:, 2 * LANES:3 * LANES] = _rope_cols(zr[:, 2 * LANES:3 * LANES], c, s1, s2)
    prow_ref[:, 3 * LANES:4 * LANES] = zr[:, 3 * LANES:4 * LANES]
    wrow_ref[:, 0:LANES] = _rope_cols(zr[:, 4 * LANES:5 * LANES], c, s1, s2)
    wrow_ref[:, LANES:2 * LANES] = zr[:, 5 * LANES:6 * LANES]
    gate_ref[...] = jax.nn.sigmoid(_dot(h, w_ref[:, qw + 6 * LANES:qw + 7 * LANES]))


def _nsa_proj(x, g, w_in, qw, tabs, tab_map, tm):
    n, d = x.shape
    row = lambda i: (i, 0)
    tspec = pl.BlockSpec((tm, LANES), tab_map)
    widths = (2 * qw, 4 * LANES, 2 * LANES, LANES)
    return pl.pallas_call(
        functools.partial(_nsa_proj_kernel, qw=qw),
        grid=(n // tm,),
        in_specs=[pl.BlockSpec((tm, d), row), pl.BlockSpec(g.shape, lambda i: (0, 0)),
                  pl.BlockSpec(w_in.shape, lambda i: (0, 0)), tspec, tspec, tspec],
        out_specs=[pl.BlockSpec((tm, w), row) for w in widths],
        out_shape=[jax.ShapeDtypeStruct((n, w), F32) for w in widths],
        compiler_params=_cp("parallel"),
        name="nsa_proj",
    )(x, g, w_in, *tabs)


def _compress_kernel(xk_ref, xv_ref, pe_ref, wk_ref, wv_ref, o_ref, *, mb):
    acck = jnp.zeros((mb, LANES), F32)
    accv = jnp.zeros((mb, LANES), F32)
    for l in range(BLK):
        ak = xk_ref[pl.ds(l, mb, stride=BLK), :] + pe_ref[l:l + 1, 0:LANES]
        av = xv_ref[pl.ds(l, mb, stride=BLK), :] + pe_ref[l:l + 1, LANES:2 * LANES]
        acck = acck + _dot(ak.astype(BF16), wk_ref[l])
        accv = accv + _dot(av.astype(BF16), wv_ref[l])
    o_ref[:, 0:LANES] = acck
    o_ref[:, LANES:2 * LANES] = accv


def _compress(rows2d, pe2, wk, wv):
    nblk = rows2d.shape[0] // BLK
    mb = _tile(nblk, 128)
    return pl.pallas_call(
        functools.partial(_compress_kernel, mb=mb),
        grid=(nblk // mb,),
        in_specs=[pl.BlockSpec((mb * BLK, LANES), lambda i: (i, 0)),
                  pl.BlockSpec((mb * BLK, LANES), lambda i: (i, 1)),
                  pl.BlockSpec(pe2.shape, lambda i: (0, 0)),
                  pl.BlockSpec(wk.shape, lambda i: (0, 0, 0)),
                  pl.BlockSpec(wv.shape, lambda i: (0, 0, 0))],
        out_specs=pl.BlockSpec((mb, 2 * LANES), lambda i: (i, 0)),
        out_shape=jax.ShapeDtypeStruct((nblk, 2 * LANES), F32),
        compiler_params=_cp("parallel"),
        name="nsa_compress",
    )(rows2d, rows2d, pe2, wk, wv)


def _compress_weights(pe, phi):
    pe2 = jnp.concatenate([pe[0], pe[0], pe[1], pe[1]], axis=1)
    ph = phi.reshape(2, BLK, HEAD_DIM, HEAD_DIM)
    z = jnp.zeros_like(ph[0])
    def bd(p):
        return jnp.concatenate([jnp.concatenate([p, z], axis=2), jnp.concatenate([z, p], axis=2)], axis=1).astype(BF16)
    return pe2, bd(ph[0]), bd(ph[1])


def _select_blocks(imp, nb):
    lane = lax.broadcasted_iota(jnp.int32, imp.shape, 1)
    cnt = jnp.zeros(imp.shape, F32)
    for i in range(nb):
        ci = imp[:, i:i + 1]
        ahead = (ci > imp) | ((ci == imp) & (lane > i))
        cnt = cnt + jnp.where(ahead, 1.0, 0.0)
    return jnp.where(cnt < float(min(TOPK, nb)), 1.0, 0.0)


def _masked_softmax(s, mask):
    s = jnp.where(mask, s, NEG)
    m = jnp.max(s, axis=-1, keepdims=True)
    e = jnp.where(mask, jnp.exp(s - m), 0.0)
    den = jnp.sum(e, axis=-1, keepdims=True)
    return e / jnp.where(den > 0, den, 1.0)


def _place(x, g):
    z = jnp.zeros_like(x)
    return jnp.concatenate([x, z] if g == 0 else [z, x], axis=1)


def _nsa_prompt_kernel(qq_ref, gate_ref, cmp_ref, e_ref, sv_ref, wv_ref, o_ref, bias_scr,
                       *, tq, kc, nb, nheads, wk):
    qi = pl.program_id(1)
    qw = nheads * HEAD_DIM
    hg = nheads // 2
    t = qi * tq + lax.broadcasted_iota(jnp.int32, (tq, 1), 0)
    qq = qq_ref[0]
    gates = gate_ref[0]
    cmp = cmp_ref[0]
    ck = cmp[:, 0:LANES].astype(BF16)
    cv = cmp[:, LANES:2 * LANES].astype(BF16)
    blk = lax.broadcasted_iota(jnp.int32, (tq, nb), 1)
    done = (blk + 1) * BLK - 1 <= t
    cur = t // BLK
    forced = (blk == cur) | (blk == 0)
    future = blk > cur
    nchunks = (qi * tq + tq - 1) // kc + 1
    wstart = pl.multiple_of(jnp.maximum(qi * tq - WINDOW, 0), tq)
    kwin = wv_ref[0, pl.ds(wstart, wk), 0:LANES].astype(BF16)
    vwin = wv_ref[0, pl.ds(wstart, wk), LANES:2 * LANES].astype(BF16)
    wpos = wstart + lax.broadcasted_iota(jnp.int32, (tq, wk), 1)
    wmask = (wpos <= t) & (wpos > t - WINDOW)
    kpos_c = lax.broadcasted_iota(jnp.int32, (tq, kc), 1)

    outs = []
    for g in range(2):
        imp = jnp.zeros((tq, nb), F32)
        o_cmp = []
        for h8 in range(hg):
            h = g * hg + h8
            qp = _place(qq[:, h * HEAD_DIM:(h + 1) * HEAD_DIM], g).astype(BF16)
            p = _masked_softmax(_dot_nt(qp, ck) * QK_SCALE, done)
            imp = imp + p
            o_cmp.append(_dot(p.astype(BF16), cv)[:, g * HEAD_DIM:(g + 1) * HEAD_DIM])
        imp = jnp.where(forced, FORCE, jnp.where(future, -1.0, imp))
        sel = _select_blocks(imp, nb).astype(BF16)

        def fill(c, _):
            allowed = (_dot(sel, e_ref[c]) > 0.5) & (c * kc + kpos_c <= t)
            bias_scr[c] = jnp.where(allowed, 0.0, NEG)
            return 0
        lax.fori_loop(0, nchunks, fill, 0)

        for h8 in range(hg):
            h = g * hg + h8
            qr = _place(qq[:, qw + h * HEAD_DIM:qw + (h + 1) * HEAD_DIM], g).astype(BF16)

            def body(c, carry):
                m, l, acc = carry
                start = pl.multiple_of(c * kc, kc)
                ks = sv_ref[0, pl.ds(start, kc), 0:LANES].astype(BF16)
                vs = sv_ref[0, pl.ds(start, kc), LANES:2 * LANES].astype(BF16)
                s = _dot_nt(qr, ks) * QK_SCALE + bias_scr[c]
                m_new = jnp.maximum(m, jnp.max(s, axis=1, keepdims=True))
                alpha = jnp.exp(m - m_new)
                e = jnp.exp(s - m_new)
                l = alpha * l + jnp.sum(e, axis=1, keepdims=True)
                acc = alpha * acc + _dot(e.astype(BF16), vs)
                return m_new, l, acc

            _, l, acc = lax.fori_loop(
                0, nchunks, body,
                (jnp.full((tq, 1), NEG, F32), jnp.zeros((tq, 1), F32), jnp.zeros((tq, LANES), F32)))
            o_slc = (acc / l)[:, g * HEAD_DIM:(g + 1) * HEAD_DIM]

            pw = _masked_softmax(_dot_nt(qr, kwin) * QK_SCALE, wmask)
            o_win = _dot(pw.astype(BF16), vwin)[:, g * HEAD_DIM:(g + 1) * HEAD_DIM]

            outs.append(gates[:, 3 * h:3 * h + 1] * o_cmp[h8] + gates[:, 3 * h + 1:3 * h + 2] * o_slc
                        + gates[:, 3 * h + 2:3 * h + 3] * o_win)
    o_ref[0] = jnp.concatenate(outs, axis=1)


def _block_expand(nb_rows, nkeys, kc):
    key = jnp.arange(nkeys).reshape(nkeys // kc, 1, kc)
    return (key // BLK == jnp.arange(nb_rows)[None, :, None]).astype(BF16)


def _nsa_prompt_attn(qq, gates, cmp, prow, wrow, nheads):
    b, t, _ = qq.shape
    qw = nheads * HEAD_DIM
    tq = _tile(t, 128)
    kc = _tile(t, 512)
    nb = t // BLK
    wk = min(t, tq + WINDOW)
    e = _block_expand(nb, t, kc)
    kern = functools.partial(_nsa_prompt_kernel, tq=tq, kc=kc, nb=nb, nheads=nheads, wk=wk)
    return pl.pallas_call(
        kern,
        grid=(b, t // tq),
        in_specs=[
            pl.BlockSpec((1, tq, 2 * qw), lambda b_, i: (b_, i, 0)),
            pl.BlockSpec((1, tq, LANES), lambda b_, i: (b_, i, 0)),
            pl.BlockSpec((1, nb, 2 * LANES), lambda b_, i: (b_, 0, 0)),
            pl.BlockSpec(e.shape, lambda b_, i: (0, 0, 0)),
            pl.BlockSpec((1, t, 2 * LANES), lambda b_, i: (b_, 0, 1)),
            pl.BlockSpec((1, t, 2 * LANES), lambda b_, i: (b_, 0, 0)),
        ],
        out_specs=pl.BlockSpec((1, tq, qw), lambda b_, i: (b_, i, 0)),
        out_shape=jax.ShapeDtypeStruct((b, t, qw), F32),
        scratch_shapes=[pltpu.VMEM((t // kc, tq, kc), F32)],
        compiler_params=_cp("parallel", "arbitrary"),
        name="nsa_prompt_attn",
    )(qq, gates, cmp, e, prow, wrow)


def _nsa_sample_kernel(pt_ref, q_ref, qr_ref, gate_ref, cnew_ref, e_ref, new_ref, win_ref, wnew_ref,
                       *rest, n_pages, page, nq, hg, nbp, past):
    pages = rest[:n_pages]
    cmps = rest[n_pages:2 * n_pages]
    o_ref = rest[2 * n_pages]
    cmp_scr = rest[2 * n_pages + 1]
    rows = 2 * nq * hg
    bpp = page // BLK
    nb = n_pages * bpp + 1
    nnew = new_ref.shape[1]
    nwin = win_ref.shape[1]

    cmp_scr[...] = jnp.zeros_like(cmp_scr)
    for p in range(n_pages):
        cmp_scr[p * bpp:(p + 1) * bpp, :] = cmps[p][0]
    cmp_scr[n_pages * bpp:n_pages * bpp + 1, :] = cnew_ref[0, 0:1, :]
    cmp = cmp_scr[...]
    ck = cmp[:, 0:LANES].astype(BF16)
    cv = cmp[:, LANES:2 * LANES].astype(BF16)

    q = q_ref[0].astype(BF16)
    qr = qr_ref[0].astype(BF16)
    gates = gate_ref[0]
    rowi = lax.broadcasted_iota(jnp.int32, (rows, 1), 0)
    t = past + (rowi // hg) % nq
    top = rowi < rows // 2

    def pick(x):
        return jnp.where(top, x[:, 0:HEAD_DIM], x[:, HEAD_DIM:2 * HEAD_DIM])

    blk = lax.broadcasted_iota(jnp.int32, (rows, nbp), 1)
    done = ((blk + 1) * BLK - 1 <= t) & (blk < nb)
    p_cmp = _masked_softmax(_dot_nt(q, ck) * QK_SCALE, done)
    o_cmp = pick(_dot(p_cmp.astype(BF16), cv))

    ng = rows // hg
    imp = jnp.sum(p_cmp.reshape(ng, hg, nbp), axis=1)
    tg = past + lax.broadcasted_iota(jnp.int32, (ng, 1), 0) % nq
    blk_g = lax.broadcasted_iota(jnp.int32, (ng, nbp), 1)
    cur = tg // BLK
    imp = jnp.where((blk_g == cur) | (blk_g == 0), FORCE, jnp.where(blk_g > cur, -1.0, imp))
    sel = _select_blocks(imp, nb)
    expand = (lax.broadcasted_iota(jnp.int32, (rows, ng), 0) // hg
              == lax.broadcasted_iota(jnp.int32, (rows, ng), 1)).astype(F32)
    sel_f = _dot(expand, sel)
    sel_rows = sel_f.astype(BF16)

    allowed = _dot(sel_rows, e_ref[...]) > 0.5
    s_parts = [_dot_nt(qr, pages[p][0][:, 0:LANES].astype(BF16)) for p in range(n_pages)]
    s = jnp.concatenate(s_parts, axis=1) * QK_SCALE
    s = jnp.where(allowed, s, NEG)
    new = new_ref[0]
    s_n = _dot_nt(qr, new[:, 0:LANES].astype(BF16)) * QK_SCALE
    npos = past + lax.broadcasted_iota(jnp.int32, (rows, nnew), 1)
    mask_n = (npos <= t) & (sel_f[:, nb - 1:nb] > 0.5)
    s_n = jnp.where(mask_n, s_n, NEG)
    m = jnp.maximum(jnp.max(s, axis=1, keepdims=True), jnp.max(s_n, axis=1, keepdims=True))
    e = jnp.where(allowed, jnp.exp(s - m), 0.0)
    e_n = jnp.where(mask_n, jnp.exp(s_n - m), 0.0)
    den = jnp.sum(e, axis=1, keepdims=True) + jnp.sum(e_n, axis=1, keepdims=True)
    acc = _dot(e_n.astype(BF16), new[:, LANES:2 * LANES].astype(BF16))
    for p in range(n_pages):
        acc = acc + _dot(e[:, p * page:(p + 1) * page].astype(BF16), pages[p][0][:, LANES:2 * LANES].astype(BF16))
    o_slc = pick(acc / jnp.where(den > 0, den, 1.0))

    win = win_ref[0]
    wnew = wnew_ref[0]
    s_w = _dot_nt(qr, win[:, 0:LANES].astype(BF16)) * QK_SCALE
    wpos = past - nwin + lax.broadcasted_iota(jnp.int32, (rows, nwin), 1)
    mask_w = (wpos <= t) & (wpos > t - WINDOW) & (wpos >= 0)
    s_w = jnp.where(mask_w, s_w, NEG)
    s_wn = _dot_nt(qr, wnew[:, 0:LANES].astype(BF16)) * QK_SCALE
    mask_wn = (npos <= t) & (npos > t - WINDOW)
    s_wn = jnp.where(mask_wn, s_wn, NEG)
    m = jnp.maximum(jnp.max(s_w, axis=1, keepdims=True), jnp.max(s_wn, axis=1, keepdims=True))
    e = jnp.where(mask_w, jnp.exp(s_w - m), 0.0)
    e_n = jnp.where(mask_wn, jnp.exp(s_wn - m), 0.0)
    den = jnp.sum(e, axis=1, keepdims=True) + jnp.sum(e_n, axis=1, keepdims=True)
    acc = _dot(e.astype(BF16), win[:, LANES:2 * LANES].astype(BF16)) \
        + _dot(e_n.astype(BF16), wnew[:, LANES:2 * LANES].astype(BF16))
    o_win = pick(acc / jnp.where(den > 0, den, 1.0))

    o_ref[0] = gates[:, 0:1] * o_cmp + gates[:, 1:2] * o_slc + gates[:, 2:3] * o_win


def _nsa_sample_attn(q, q_rot, gates, cmp_pool, cmp_new, cache, new_blk, win_state, wnew, page_table, nq):
    db, rows, _ = q.shape
    n_pages = page_table.shape[1]
    page = cache.shape[1]
    past = n_pages * page
    nb = n_pages * (page // BLK) + 1
    nbp = -(-nb // 8) * 8
    nkeys = n_pages * page
    e = (jnp.arange(nkeys)[None, :] // BLK == jnp.arange(nbp)[:, None]).astype(BF16)
    nnew = new_blk.shape[1]
    kern = functools.partial(_nsa_sample_kernel, n_pages=n_pages, page=page, nq=nq, hg=rows // (2 * nq),
                             nbp=nbp, past=past)

    bpp = page // BLK
    cmp_pool3 = cmp_pool.reshape(cmp_pool.shape[0] // bpp, bpp, 2 * LANES)

    def page_spec(p):
        return pl.BlockSpec((1, page, 2 * LANES), lambda b, pt: (pt[b, p], 0, 1))

    def cmp_spec(p):
        return pl.BlockSpec((1, bpp, 2 * LANES), lambda b, pt: (pt[b, p], 0, 0))

    grid_spec = pltpu.PrefetchScalarGridSpec(
        num_scalar_prefetch=1,
        grid=(db,),
        in_specs=[
            pl.BlockSpec((1, rows, LANES), lambda b, pt: (b, 0, 0)),
            pl.BlockSpec((1, rows, LANES), lambda b, pt: (b, 0, 0)),
            pl.BlockSpec((1, rows, 3), lambda b, pt: (b, 0, 0)),
            pl.BlockSpec((1, 1, 2 * LANES), lambda b, pt: (b, 0, 0)),
            pl.BlockSpec(e.shape, lambda b, pt: (0, 0)),
            pl.BlockSpec((1, nnew, 2 * LANES), lambda b, pt: (b, 0, 1)),
            pl.BlockSpec((1, win_state.shape[1], 2 * LANES), lambda b, pt: (b, 0, 0)),
            pl.BlockSpec((1, nnew, 2 * LANES), lambda b, pt: (b, 0, 0)),
        ] + [page_spec(p) for p in range(n_pages)] + [cmp_spec(p) for p in range(n_pages)],
        out_specs=pl.BlockSpec((1, rows, HEAD_DIM), lambda b, pt: (b, 0, 0)),
        scratch_shapes=[pltpu.VMEM((nbp, 2 * LANES), F32)],
    )
    return pl.pallas_call(
        kern,
        grid_spec=grid_spec,
        out_shape=jax.ShapeDtypeStruct((db, rows, HEAD_DIM), F32),
        compiler_params=_cp("arbitrary"),
        name="nsa_sample_attn",
    )(page_table, q, q_rot, gates, cmp_new, e, new_blk, win_state, wnew,
      *([cache] * n_pages), *([cmp_pool3] * n_pages))


def kernel(x_prompt, x_sample, mem_prompt, page_table, cache_kv_l0, cache_mem_l0, cache_nsa_l1, state_win_l1,
           cache_mem_l1, cache_kv_l2, cache_mem_l2, cache_nsa_l3, state_win_l3, cache_mem_l3, norm_g, mem_norm_g,
           ffn_w_in, ffn_w_out, w_q_mem, w_kv_mem, w_o_mem, ab_w_in, ab_w_out, diff_lambda, diff_subln_g,
           nsa_w_in, nsa_w_out, nsa_cmp_pe, nsa_cmp_phi):
    bsz, seq, d = x_prompt.shape
    db, dq, _ = x_sample.shape
    depth = norm_g.shape[0]
    n_pages = page_table.shape[1]
    page = cache_kv_l0.shape[1]
    past = n_pages * page
    mlen = mem_prompt.shape[1]
    mem_w = w_q_mem.shape[2]
    mem_heads = mem_w // LANES
    nheads = d // HEAD_DIM
    qw = nheads * HEAD_DIM
    kv_caches = (cache_kv_l0, cache_kv_l2)
    nsa_caches = (cache_nsa_l1, cache_nsa_l3)
    win_states = (state_win_l1, state_win_l3)
    mem_caches = (cache_mem_l0, cache_mem_l1, cache_mem_l2, cache_mem_l3)

    np_, ns_ = bsz * seq, db * dq
    tm_p = _tile(seq, 512)
    tm_s = ns_
    tabs_p = _rope_tables(jnp.arange(seq, dtype=jnp.int32))
    tabs_s = _rope_tables(past + jnp.arange(ns_, dtype=jnp.int32) % dq)
    spt = seq // tm_p
    map_p = lambda i: (i % spt, 0)
    map_s = lambda i: (0, 0)

    bf = lambda w: w.astype(BF16)
    ffn_in, ffn_out = bf(ffn_w_in), bf(ffn_w_out)
    wq_m, wkv_m, wo_m = bf(w_q_mem), bf(w_kv_mem), bf(w_o_mem)
    ab_in, ab_out = bf(ab_w_in), bf(ab_w_out)
    nsa_in = bf(jnp.pad(nsa_w_in, ((0, 0), (0, 0), (0, qw + 7 * LANES - nsa_w_in.shape[2]))))
    nsa_out = bf(nsa_w_out)

    mkv_all = _mem_kv(mem_prompt.reshape(bsz * mlen, d), mem_norm_g.reshape(depth, 1, d), wkv_m)

    xp = x_prompt.reshape(np_, d)
    xs = x_sample.reshape(ns_, d)
    kv_p, kv_s, nsa_p, nsa_s, win_p, win_s, mem_p = [], [], [], [], [], [], []
    om_p = om_s = None
    for layer in range(depth):
        g = norm_g[layer]
        i = layer // 2
        if layer == 0:
            xp = _ffn_half(xp, g, ffn_in[layer, 0], ffn_out[layer, 0], 0, 1)
            xs = _ffn_half(xs, g, ffn_in[layer, 0], ffn_out[layer, 0], 0, 1)
        if layer % 2 == 0:
            lam_init = 0.8 - 0.6 * math.exp(-0.3 * layer)
            lam = diff_lambda[i]
            sg = diff_subln_g[i]
            qz_p, kvr_p = _ab_proj(xp, g, ab_in[i], tabs_p, map_p, tm_p)
            qz_s, kvr_s = _ab_proj(xs, g, ab_in[i], tabs_s, map_s, tm_s)
            kvw = kvr_p.shape[1]
            mp = _ab_prompt_attn(qz_p.reshape(bsz, seq, -1), kvr_p.reshape(bsz, seq, kvw), lam, sg, lam_init)
            ms = _ab_sample_attn(qz_s.reshape(db, dq, -1), kvr_s.reshape(db, dq, kvw), kv_caches[i], page_table,
                                 lam, sg, lam_init, npg=_tile(n_pages, 4))
            kv_p.append(kvr_p.reshape(bsz, seq, 2, kvw // (2 * HEAD_DIM), HEAD_DIM))
            kv_s.append(kvr_s.reshape(db, dq, 2, kvw // (2 * HEAD_DIM), HEAD_DIM))
            mp = mp.reshape(np_, -1)
            ms = ms.reshape(ns_, -1)
            w_mix = ab_out[i]
        else:
            qq_p, prow_p, wrow_p, gate_p = _nsa_proj(xp, g, nsa_in[i], qw, tabs_p, map_p, tm_p)
            qq_s, prow_s, wrow_s, gate_s = _nsa_proj(xs, g, nsa_in[i], qw, tabs_s, map_s, tm_s)
            pe2, wk, wv = _compress_weights(nsa_cmp_pe[i], nsa_cmp_phi[i])
            cmp_p = _compress(prow_p, pe2, wk, wv).reshape(bsz, seq // BLK, 2 * LANES)
            mp = _nsa_prompt_attn(qq_p.reshape(bsz, seq, -1), gate_p.reshape(bsz, seq, LANES), cmp_p,
                                  prow_p.reshape(bsz, seq, -1), wrow_p.reshape(bsz, seq, -1), nheads)
            mp = mp.reshape(np_, -1)
            cache = nsa_caches[i]
            n_pool = cache.shape[0]
            roww = prow_s.shape[1]
            cache3 = cache.reshape(n_pool, page, roww)
            cmp_pool = _compress(cache3.reshape(n_pool * page, roww), pe2, wk, wv)
            new_blk = jnp.pad(prow_s.reshape(db, dq, roww), ((0, 0), (0, BLK - dq), (0, 0)))
            cmp_new = _compress(new_blk.reshape(db * BLK, roww), pe2, wk, wv).reshape(db, 1, 2 * LANES)
            wnew = jnp.pad(wrow_s.reshape(db, dq, -1), ((0, 0), (0, BLK - dq), (0, 0)))
            win_state = win_states[i].reshape(db, win_states[i].shape[1], -1)
            hg = nheads // 2

            def to_rows(a, pad):
                w = a.shape[1] // nheads
                a = a.reshape(db, dq, 2, hg, w).transpose(0, 2, 1, 3, 4)
                if pad:
                    z = jnp.zeros_like(a[:, 0:1])
                    a = jnp.concatenate([jnp.concatenate([a[:, 0:1], z], axis=-1),
                                         jnp.concatenate([z, a[:, 1:2]], axis=-1)], axis=1)
                return a.reshape(db, 2 * dq * hg, -1)

            q_rows = to_rows(qq_s[:, 0:qw], True)
            qr_rows = to_rows(qq_s[:, qw:2 * qw], True)
            g_rows = to_rows(gate_s[:, 0:3 * nheads], False)
            o_rows = _nsa_sample_attn(q_rows, qr_rows, g_rows, cmp_pool, cmp_new, cache3, new_blk, win_state, wnew,
                                      page_table, dq)
            ms = o_rows.reshape(db, 2, dq, hg, HEAD_DIM).transpose(0, 2, 1, 3, 4).reshape(ns_, qw)
            nsa_p.append(prow_p.reshape(bsz, seq, 4, 2, HEAD_DIM))
            nsa_s.append(prow_s.reshape(db, dq, 4, 2, HEAD_DIM))
            n_keep = min(WINDOW, seq)
            win_p.append(wrow_p.reshape(bsz, seq, 2, 2, HEAD_DIM)[:, seq - n_keep:])
            win_all = jnp.concatenate([win_states[i], wrow_s.reshape(db, dq, 2, 2, HEAD_DIM)], axis=1)
            win_s.append(win_all[:, dq:])
            w_mix = nsa_out[i]

        xp, qm_p = _mix_out(mp, xp, g, w_mix, wq_m[layer])
        xs, qm_s = _mix_out(ms, xs, g, w_mix, wq_m[layer])
        mkv = mkv_all[layer].reshape(bsz, mlen, 2 * mem_w)
        om_p = _mem_attn(qm_p.reshape(bsz, seq, mem_w), mkv, mem_heads).reshape(np_, mem_w)
        om_s = _mem_attn(qm_s.reshape(db, dq, mem_w), mem_caches[layer].reshape(db, mlen, 2 * mem_w),
                         mem_heads).reshape(ns_, mem_w)
        mem_p.append(mkv.reshape(bsz, mlen, 2, mem_heads, LANES))
        xp = _ffn_half(xp, g, ffn_in[layer, 1], ffn_out[layer, 1], 6, 7, om_p, wo_m[layer], 5)
        xs = _ffn_half(xs, g, ffn_in[layer, 1], ffn_out[layer, 1], 6, 7, om_s, wo_m[layer], 5)
        if layer + 1 < depth:
            gn = norm_g[layer + 1]
            xp = _ffn_half(xp, gn, ffn_in[layer + 1, 0], ffn_out[layer + 1, 0], 0, 1)
            xs = _ffn_half(xs, gn, ffn_in[layer + 1, 0], ffn_out[layer + 1, 0], 0, 1)

    return (xp.reshape(bsz, seq, d), xs.reshape(db, dq, d),
            kv_p[0], kv_s[0], mem_p[0],
            nsa_p[0], nsa_s[0], win_p[0], win_s[0], mem_p[1],
            kv_p[1], kv_s[1], mem_p[2],
            nsa_p[1], nsa_s[1], win_p[1], win_s[1], mem_p[3])
```

```python
import functools
import math

import jax
import jax.numpy as jnp
from jax import lax
from jax.experimental import pallas as pl
from jax.experimental.pallas import tpu as pltpu

F32 = jnp.float32
BF16 = jnp.bfloat16

HEAD_DIM = 64
ROT_DIM = HEAD_DIM // 4
ROPE_THETA = 500000.0
BLK = 64
TOPK = 8
WINDOW = 256
EPS = 1e-6
NEG = -1e30
FORCE = 1e4
LANES = 128
QK_SCALE = 1.0 / math.sqrt(HEAD_DIM)

VMEM_LIMIT = 56 * 1024 * 1024


def _cp(*sem):
    return pltpu.CompilerParams(dimension_semantics=sem, vmem_limit_bytes=VMEM_LIMIT)


def _rms(x, g):
    return x * lax.rsqrt(jnp.mean(x * x, axis=-1, keepdims=True) + EPS) * g


def _dot(a, b):
    return jnp.dot(a, b, preferred_element_type=F32)


def _dot_nt(a, b):
    return lax.dot_general(a, b, (((1,), (1,)), ((), ())), preferred_element_type=F32)


def _softplus(z):
    return jnp.maximum(z, 0.0) + jnp.log1p(jnp.exp(-jnp.abs(z)))


def _split_bf16(x):
    hi = x.astype(BF16)
    lo = (x - hi.astype(F32)).astype(BF16)
    return hi, lo


def _tile(n, pref):
    t = min(n, pref)
    while n % t:
        t //= 2
    return t


def _rope_tables(pos):
    half = ROT_DIM // 2
    inv = jnp.power(ROPE_THETA, -jnp.arange(half, dtype=F32) * 2.0 / ROT_DIM)
    ang = pos.astype(F32)[:, None] * inv[None, :]
    cos = jnp.cos(ang)
    sin = jnp.sin(ang)
    p = pos.shape[0]
    ones = jnp.ones((p, HEAD_DIM - ROT_DIM), F32)
    zeros = jnp.zeros((p, HEAD_DIM - ROT_DIM), F32)
    zh = jnp.zeros((p, half), F32)
    c = jnp.concatenate([cos, cos, ones], axis=1)
    s1 = jnp.concatenate([-sin, zh, zeros], axis=1)
    s2 = jnp.concatenate([zh, sin, zeros], axis=1)
    rep = LANES // HEAD_DIM
    return jnp.tile(c, (1, rep)), jnp.tile(s1, (1, rep)), jnp.tile(s2, (1, rep))


def _rope_cols(z, c, s1, s2):
    outs = []
    for i in range(z.shape[1] // LANES):
        xb = z[:, i * LANES:(i + 1) * LANES]
        outs.append(xb * c + pltpu.roll(xb, LANES - ROT_DIM // 2, 1) * s1 + pltpu.roll(xb, ROT_DIM // 2, 1) * s2)
    return outs[0] if len(outs) == 1 else jnp.concatenate(outs, axis=1)


def _ffn_kernel(*refs, nj, pre, ipre, ipost, imem):
    if pre:
        x_ref, om_ref, wom_ref, g_ref, wg_ref, wu_ref, wo_ref, o_ref, h_scr, acc_scr, x_scr = refs
    else:
        x_ref, g_ref, wg_ref, wu_ref, wo_ref, o_ref, h_scr, acc_scr, x_scr = refs
    j = pl.program_id(1)

    @pl.when(j == 0)
    def _():
        x = x_ref[...]
        if pre:
            y = _dot(om_ref[...].astype(BF16), wom_ref[...])
            x = x + _rms(y, g_ref[imem:imem + 1, :])
        x_scr[...] = x
        h_scr[...] = _rms(x, g_ref[ipre:ipre + 1, :]).astype(BF16)
        acc_scr[...] = jnp.zeros_like(acc_scr)

    h = h_scr[...]
    gate = _dot(h, wg_ref[...])
    up = _dot(h, wu_ref[...])
    act = (gate * jax.nn.sigmoid(gate)) * up
    acc_scr[...] += _dot(act.astype(BF16), wo_ref[...])

    @pl.when(j == nj - 1)
    def _():
        o_ref[...] = x_scr[...] + 0.5 * _rms(acc_scr[...], g_ref[ipost:ipost + 1, :])


def _ffn_half(x, g, w_in, w_out, ipre, ipost, om=None, w_om=None, imem=None):
    n, d = x.shape
    dff = w_out.shape[0]
    tm = _tile(n, 512)
    tf = dff // 2 if (dff // 2) % LANES == 0 else dff
    nj = dff // tf
    pre = om is not None
    kern = functools.partial(_ffn_kernel, nj=nj, pre=pre, ipre=ipre, ipost=ipost, imem=imem)
    row = lambda i, j: (i, 0)
    in_specs = [pl.BlockSpec((tm, d), row)]
    args = [x]
    if pre:
        in_specs += [pl.BlockSpec((tm, om.shape[1]), row), pl.BlockSpec(w_om.shape, lambda i, j: (0, 0))]
        args += [om, w_om]
    in_specs += [
        pl.BlockSpec(g.shape, lambda i, j: (0, 0)),
        pl.BlockSpec((d, tf), lambda i, j: (0, j)),
        pl.BlockSpec((d, tf), lambda i, j: (0, nj + j)),
        pl.BlockSpec((tf, d), lambda i, j: (j, 0)),
    ]
    args += [g, w_in, w_in, w_out]
    return pl.pallas_call(
        kern,
        grid=(n // tm, nj),
        in_specs=in_specs,
        out_specs=pl.BlockSpec((tm, d), row),
        out_shape=jax.ShapeDtypeStruct((n, d), F32),
        scratch_shapes=[pltpu.VMEM((tm, d), BF16), pltpu.VMEM((tm, d), F32), pltpu.VMEM((tm, d), F32)],
        compiler_params=_cp("parallel", "arbitrary"),
        name="ffn_half",
    )(*args)


def _ab_proj_kernel(x_ref, g_ref, w_ref, c_ref, s1_ref, s2_ref, qz_ref, kv_ref, *, sbw):
    h = _rms(x_ref[...], g_ref[2:3, :]).astype(BF16)
    c, s1, s2 = c_ref[...], s1_ref[...], s2_ref[...]

    def proj(i):
        return _dot(h, w_ref[:, i * sbw:(i + 1) * sbw])

    qz_ref[:, 0:sbw] = proj(0)
    kv_ref[:, 0:sbw] = proj(1)
    kv_ref[:, 2 * sbw:3 * sbw] = proj(2)
    qz_ref[:, sbw:2 * sbw] = _rope_cols(proj(3), c, s1, s2)
    kv_ref[:, sbw:2 * sbw] = _rope_cols(proj(4), c, s1, s2)
    kv_ref[:, 3 * sbw:4 * sbw] = proj(5)


def _ab_proj(x, g, w_in, tabs, tab_map, tm):
    n, d = x.shape
    sbw = w_in.shape[1] // 6
    row = lambda i: (i, 0)
    tspec = pl.BlockSpec((tm, LANES), tab_map)
    return pl.pallas_call(
        functools.partial(_ab_proj_kernel, sbw=sbw),
        grid=(n // tm,),
        in_specs=[pl.BlockSpec((tm, d), row), pl.BlockSpec(g.shape, lambda i: (0, 0)),
                  pl.BlockSpec(w_in.shape, lambda i: (0, 0)), tspec, tspec, tspec],
        out_specs=[pl.BlockSpec((tm, 2 * sbw), row), pl.BlockSpec((tm, 4 * sbw), row)],
        out_shape=[jax.ShapeDtypeStruct((n, 2 * sbw), F32), jax.ShapeDtypeStruct((n, 4 * sbw), F32)],
        compiler_params=_cp("parallel"),
        name="ab_proj",
    )(x, g, w_in, *tabs)


def _diff_lambda(lam_ref, lam_init):
    lv = lam_ref[...]
    a = jnp.sum(lv[0:1, :] * lv[1:2, :], axis=1, keepdims=True)
    b = jnp.sum(lv[2:3, :] * lv[3:4, :], axis=1, keepdims=True)
    return jnp.exp(a) - jnp.exp(b) + lam_init


def _ab_prompt_kernel(lam_ref, sg_ref, uu_ref, q_ref, k_ref, v_ref, o_ref, *, tq, n_sb, lam_init):
    u = pl.program_id(1)
    qi = pl.program_id(2)
    q = q_ref[0]
    lane_half = lax.broadcasted_iota(jnp.int32, (tq, LANES), 1) // HEAD_DIM
    qs = jnp.concatenate([jnp.where(lane_half == 0, q, 0.0), jnp.where(lane_half == 1, q, 0.0)], axis=0).astype(BF16)
    row = lax.broadcasted_iota(jnp.int32, (2 * tq, tq), 0) % tq
    col = lax.broadcasted_iota(jnp.int32, (2 * tq, tq), 1)

    def load_kv(kb):
        start = pl.multiple_of(kb * tq, tq)
        return k_ref[0, pl.ds(start, tq), :].astype(BF16), v_ref[0, pl.ds(start, tq), :].astype(BF16)

    @pl.when(u < n_sb)
    def _():
        uu = uu_ref[...]

        def step(kb, run, acc, mask):
            k, v = load_kv(kb)
            z = _dot_nt(qs, k) * QK_SCALE
            sp = _softplus(z)
            lk = -sp if mask is None else jnp.where(mask, -sp, 0.0)
            hi, lo = _split_bf16(lk)
            later = _dot(jnp.concatenate([hi, lo], axis=1), uu) + run
            w = jnp.exp(z - sp + later)
            if mask is not None:
                w = jnp.where(mask, w, 0.0)
            acc = acc + _dot(w.astype(BF16), v)
            run = run + jnp.sum(lk, axis=1, keepdims=True)
            return run, acc

        run, acc = step(qi, jnp.zeros((2 * tq, 1), F32), jnp.zeros((2 * tq, LANES), F32), col < row)
        _, acc = lax.fori_loop(1, qi + 1, lambda i, c: step(qi - i, c[0], c[1], None), (run, acc))
        o_ref[0] = jnp.where(lane_half == 0, acc[0:tq], acc[tq:2 * tq])

    @pl.when(u >= n_sb)
    def _():
        lam = _diff_lambda(lam_ref, lam_init)

        def step(kb, m, l, acc, mask):
            k, v = load_kv(kb)
            s = _dot_nt(qs, k) * QK_SCALE
            if mask is not None:
                s = jnp.where(mask, s, NEG)
            m_new = jnp.maximum(m, jnp.max(s, axis=1, keepdims=True))
            alpha = jnp.exp(m - m_new)
            e = jnp.exp(s - m_new)
            if mask is not None:
                e = jnp.where(mask, e, 0.0)
            l = alpha * l + jnp.sum(e, axis=1, keepdims=True)
            acc = alpha * acc + _dot(e.astype(BF16), v)
            return m_new, l, acc

        m, l, acc = step(qi, jnp.full((2 * tq, 1), NEG, F32), jnp.zeros((2 * tq, 1), F32),
                         jnp.zeros((2 * tq, LANES), F32), col <= row)
        m, l, acc = lax.fori_loop(0, qi, lambda kb, c: step(kb, c[0], c[1], c[2], None), (m, l, acc))
        p = acc / l
        o = p[0:tq] - lam * p[tq:2 * tq]
        o_ref[0] = _rms(o, sg_ref[...]) * (1.0 - lam_init)


def _cumsum_matrix(tk):
    j = jnp.arange(2 * tk)[:, None] % tk
    s = jnp.arange(tk)[None, :]
    return (j > s).astype(BF16)


def _ab_prompt_attn(qz, kv, lam, subln_g, lam_init):
    b, t, w = qz.shape
    nu = w // LANES
    tq = _tile(t, 256)
    kern = functools.partial(_ab_prompt_kernel, tq=tq, n_sb=nu // 2, lam_init=lam_init)
    return pl.pallas_call(
        kern,
        grid=(b, nu, t // tq),
        in_specs=[
            pl.BlockSpec(lam.shape, lambda b_, u, i: (0, 0)),
            pl.BlockSpec((1, LANES), lambda b_, u, i: (0, 0)),
            pl.BlockSpec((2 * tq, tq), lambda b_, u, i: (0, 0)),
            pl.BlockSpec((1, tq, LANES), lambda b_, u, i: (b_, i, u)),
            pl.BlockSpec((1, t, LANES), lambda b_, u, i: (b_, 0, u)),
            pl.BlockSpec((1, t, LANES), lambda b_, u, i: (b_, 0, nu + u)),
        ],
        out_specs=pl.BlockSpec((1, tq, LANES), lambda b_, u, i: (b_, i, u)),
        out_shape=jax.ShapeDtypeStruct((b, t, w), F32),
        compiler_params=_cp("parallel", "parallel", "arbitrary"),
        name="ab_prompt_attn",
    )(lam, subln_g.reshape(1, LANES), _cumsum_matrix(tq), qz, kv, kv)


def _ab_sample_kernel(pt_ref, lam_ref, sg_ref, uu_ref, q_ref, kvn_ref, *rest, npg, page, nq, past, lam_init):
    pages = rest[:npg]
    o_ref = rest[npg]
    qbd_f, qbd, newpage, o_acc, run_s, m_s, l_s = rest[npg + 1:]
    j = pl.program_id(1)
    nj = pl.num_programs(1)
    nu = q_ref.shape[2] // LANES
    rows = nu * 2 * nq
    hr = rows // 2
    kw = nu * LANES

    @pl.when(j == 0)
    def _():
        q = q_ref[0]
        lane_half = lax.broadcasted_iota(jnp.int32, (nq, LANES), 1) // HEAD_DIM
        qbd_f[...] = jnp.zeros_like(qbd_f)
        for u in range(nu):
            qu = q[:, u * LANES:(u + 1) * LANES]
            piece = jnp.concatenate([jnp.where(lane_half == 0, qu, 0.0), jnp.where(lane_half == 1, qu, 0.0)], axis=0)
            qbd_f[u * 2 * nq:(u + 1) * 2 * nq, u * LANES:(u + 1) * LANES] = piece
        qbd[...] = qbd_f[...].astype(BF16)
        newpage[...] = jnp.zeros_like(newpage)
        newpage[0:nq, :] = kvn_ref[0]
        o_acc[...] = jnp.zeros_like(o_acc)
        run_s[...] = jnp.zeros_like(run_s)
        m_s[...] = jnp.full_like(m_s, NEG)
        l_s[...] = jnp.zeros_like(l_s)

    lane = lax.broadcasted_iota(jnp.int32, (hr, page), 1)
    qpos = past + lax.broadcasted_iota(jnp.int32, (hr, page), 0) % nq
    uu = uu_ref[...]

    def process(scores, weighted_values, page_start):
        s = scores(qbd[...]) * QK_SCALE
        kpos = page_start + lane
        z = s[0:hr]
        mask = kpos < qpos
        sp = _softplus(z)
        lk = jnp.where(mask, -sp, 0.0)
        hi, lo = _split_bf16(lk)
        later = _dot(jnp.concatenate([hi, lo], axis=1), uu) + run_s[...]
        w = jnp.where(mask, jnp.exp(z - sp + later), 0.0)
        run_s[...] += jnp.sum(lk, axis=1, keepdims=True)
        sd = s[hr:rows]
        maskd = kpos <= qpos
        sd = jnp.where(maskd, sd, NEG)
        m_old = m_s[...]
        m_new = jnp.maximum(m_old, jnp.max(sd, axis=1, keepdims=True))
        alpha = jnp.exp(m_old - m_new)
        e = jnp.where(maskd, jnp.exp(sd - m_new), 0.0)
        l_s[...] = alpha * l_s[...] + jnp.sum(e, axis=1, keepdims=True)
        m_s[...] = m_new
        pv = weighted_values(jnp.concatenate([w, e], axis=0).astype(BF16))
        o_acc[0:hr, :] += pv[0:hr]
        o_acc[hr:rows, :] = alpha * o_acc[hr:rows, :] + pv[hr:rows]

    @pl.when(j == 0)
    def _():
        process(lambda q: _dot_nt(q, newpage[:, 0:kw].astype(BF16)),
                lambda w: _dot(w, newpage[:, kw:2 * kw].astype(BF16)), past)

    for p in range(npg):
        pidx = (nj * npg - 1) - (j * npg + p)
        pg = pages[p]
        process(lambda q, pg=pg: _dot(q, pg[0, 0].astype(BF16)),
                lambda w, pg=pg: _dot_nt(w, pg[0, 1].astype(BF16)), pidx * page)

    @pl.when(j == nj - 1)
    def _():
        lam = _diff_lambda(lam_ref, lam_init)
        lane_half = lax.broadcasted_iota(jnp.int32, (nq, LANES), 1) // HEAD_DIM
        outs = []
        for u in range(nu):
            blk = o_acc[u * 2 * nq:(u + 1) * 2 * nq, u * LANES:(u + 1) * LANES]
            if u < nu // 2:
                outs.append(jnp.where(lane_half == 0, blk[0:nq], blk[nq:2 * nq]))
            else:
                r0 = u * 2 * nq - hr
                l = l_s[r0:r0 + 2 * nq, :]
                l = jnp.where(l > 0, l, 1.0)
                o = blk[0:nq] / l[0:nq] - lam * (blk[nq:2 * nq] / l[nq:2 * nq])
                outs.append(_rms(o, sg_ref[...]) * (1.0 - lam_init))
        o_ref[0] = jnp.concatenate(outs, axis=1)


def _ab_sample_attn(qz, kv_new, cache, page_table, lam, subln_g, lam_init, npg):
    db, nq, w = qz.shape
    n_pages = page_table.shape[1]
    page = cache.shape[1]
    kvw = kv_new.shape[2]
    nu = w // LANES
    rows = nu * 2 * nq
    past = n_pages * page
    nj = n_pages // npg
    cache_t = jnp.transpose(cache, (0, 2, 3, 4, 1)).reshape(cache.shape[0], 2, kvw // 2, page)
    kern = functools.partial(_ab_sample_kernel, npg=npg, page=page, nq=nq, past=past, lam_init=lam_init)

    def page_spec(p):
        return pl.BlockSpec((1, 2, kvw // 2, page),
                            lambda b, j, pt: (pt[b, n_pages - 1 - (j * npg + p)], 0, 0, 0))

    grid_spec = pltpu.PrefetchScalarGridSpec(
        num_scalar_prefetch=1,
        grid=(db, nj),
        in_specs=[
            pl.BlockSpec(lam.shape, lambda b, j, pt: (0, 0)),
            pl.BlockSpec((1, LANES), lambda b, j, pt: (0, 0)),
            pl.BlockSpec((2 * page, page), lambda b, j, pt: (0, 0)),
            pl.BlockSpec((1, nq, w), lambda b, j, pt: (b, 0, 0)),
            pl.BlockSpec((1, nq, kvw), lambda b, j, pt: (b, 0, 0)),
        ] + [page_spec(p) for p in range(npg)],
        out_specs=pl.BlockSpec((1, nq, w), lambda b, j, pt: (b, 0, 0)),
        scratch_shapes=[
            pltpu.VMEM((rows, w), F32), pltpu.VMEM((rows, w), BF16), pltpu.VMEM((page, kvw), F32),
            pltpu.VMEM((rows, w), F32), pltpu.VMEM((rows // 2, 1), F32),
            pltpu.VMEM((rows // 2, 1), F32), pltpu.VMEM((rows // 2, 1), F32),
        ],
    )
    return pl.pallas_call(
        kern,
        grid_spec=grid_spec,
        out_shape=jax.ShapeDtypeStruct((db, nq, w), F32),
        compiler_params=_cp("parallel", "arbitrary"),
        name="ab_sample_attn",
    )(page_table, lam, subln_g.reshape(1, LANES), _cumsum_matrix(page), qz, kv_new, *([cache_t] * npg))


def _mix_out_kernel(m_ref, x_ref, g_ref, wo_ref, wq_ref, x1_ref, qm_ref):
    y = _dot(m_ref[...].astype(BF16), wo_ref[...])
    x1 = x_ref[...] + _rms(y, g_ref[3:4, :])
    x1_ref[...] = x1
    qm_ref[...] = _dot(_rms(x1, g_ref[4:5, :]).astype(BF16), wq_ref[...])


def _mix_out(m, x, g, w_out, w_q):
    n, d = x.shape
    tm = _tile(n, 512)
    row = lambda i: (i, 0)
    full = lambda i: (0, 0)
    return pl.pallas_call(
        _mix_out_kernel,
        grid=(n // tm,),
        in_specs=[pl.BlockSpec((tm, m.shape[1]), row), pl.BlockSpec((tm, d), row), pl.BlockSpec(g.shape, full),
                  pl.BlockSpec(w_out.shape, full), pl.BlockSpec(w_q.shape, full)],
        out_specs=[pl.BlockSpec((tm, d), row), pl.BlockSpec((tm, w_q.shape[1]), row)],
        out_shape=[jax.ShapeDtypeStruct((n, d), F32), jax.ShapeDtypeStruct((n, w_q.shape[1]), F32)],
        compiler_params=_cp("parallel"),
        name="mix_out",
    )(m, x, g, w_out, w_q)


def _mem_attn_kernel(q_ref, kv_ref, o_ref, *, nh, hd, interleaved):
    q = q_ref[0]
    scale = 1.0 / math.sqrt(hd)
    outs = []
    for h in range(nh):
        qh = q[:, h * hd:(h + 1) * hd].astype(BF16)
        if interleaved:
            mlen = kv_ref.shape[1] // (2 * nh)
            kh = kv_ref[0, pl.ds(h, mlen, stride=2 * nh), :].astype(BF16)
            vh = kv_ref[0, pl.ds(nh + h, mlen, stride=2 * nh), :].astype(BF16)
        else:
            kh = kv_ref[0, :, h * hd:(h + 1) * hd].astype(BF16)
            vh = kv_ref[0, :, (nh + h) * hd:(nh + h + 1) * hd].astype(BF16)
        s = _dot_nt(qh, kh) * scale
        m = jnp.max(s, axis=1, keepdims=True)
        e = jnp.exp(s - m)
        p = e / jnp.sum(e, axis=1, keepdims=True)
        outs.append(_dot(p.astype(BF16), vh))
    o_ref[0] = jnp.concatenate(outs, axis=1)


def _mem_attn(q, kv, nh, interleaved=False):
    b, t, w = q.shape
    tq = _tile(t, 512)
    return pl.pallas_call(
        functools.partial(_mem_attn_kernel, nh=nh, hd=w // nh, interleaved=interleaved),
        grid=(b, t // tq),
        in_specs=[pl.BlockSpec((1, tq, w), lambda b_, i: (b_, i, 0)),
                  pl.BlockSpec((1,) + kv.shape[1:], lambda b_, i: (b_, 0, 0))],
        out_specs=pl.BlockSpec((1, tq, w), lambda b_, i: (b_, i, 0)),
        out_shape=jax.ShapeDtypeStruct((b, t, w), F32),
        compiler_params=_cp("parallel", "arbitrary"),
        name="mem_attn",
    )(q, kv)


def _mem_kv_kernel(x_ref, g_ref, w_ref, o_ref):
    o_ref[0] = _dot(_rms(x_ref[...], g_ref[0]).astype(BF16), w_ref[0])


def _mem_kv(mem, g, w_kv):
    n, d = mem.shape
    nl, _, wo = w_kv.shape
    tm = _tile(n, 512)
    return pl.pallas_call(
        _mem_kv_kernel,
        grid=(nl, n // tm),
        in_specs=[pl.BlockSpec((tm, d), lambda l, i: (i, 0)), pl.BlockSpec((1, 1, d), lambda l, i: (l, 0, 0)),
                  pl.BlockSpec((1, d, wo), lambda l, i: (l, 0, 0))],
        out_specs=pl.BlockSpec((1, tm, wo), lambda l, i: (l, i, 0)),
        out_shape=jax.ShapeDtypeStruct((nl, n, wo), F32),
        compiler_params=_cp("parallel", "parallel"),
        name="mem_kv",
    )(mem, g, w_kv)


def _nsa_proj_kernel(x_ref, g_ref, w_ref, c_ref, s1_ref, s2_ref, qq_ref, prow_ref, wrow_ref, gate_ref, *, qw):
    h = _rms(x_ref[...], g_ref[2:3, :]).astype(BF16)
    c, s1, s2 = c_ref[...], s1_ref[...], s2_ref[...]
    hw = qw // 2
    for i in range(2):
        z = _dot(h, w_ref[:, i * hw:(i + 1) * hw])
        qq_ref[:, i * hw:(i + 1) * hw] = z
        qq_ref[:, qw + i * hw:qw + (i + 1) * hw] = _rope_cols(z, c, s1, s2)
    zr = _dot(h, w_ref[:, qw:qw + 6 * LANES])
    prow_ref[:, 0:2 * LANES] = zr[:, 0:2 * LANES]
    prow_ref[:, 2 * LANES:3 * LANES] = _rope_cols(zr[:, 2 * LANES:3 * LANES], c, s1, s2)
    prow_ref[:, 3 * LANES:4 * LANES] = zr[:, 3 * LANES:4 * LANES]
    wrow_ref[:, 0:LANES] = _rope_cols(zr[:, 4 * LANES:5 * LANES], c, s1, s2)
    wrow_ref[:, LANES:2 * LANES] = zr[:, 5 * LANES:6 * LANES]
    gate_ref[...] = jax.nn.sigmoid(_dot(h, w_ref[:, qw + 6 * LANES:qw + 7 * LANES]))


def _nsa_proj(x, g, w_in, qw, tabs, tab_map, tm):
    n, d = x.shape
    row = lambda i: (i, 0)
    tspec = pl.BlockSpec((tm, LANES), tab_map)
    widths = (2 * qw, 4 * LANES, 2 * LANES, LANES)
    return pl.pallas_call(
        functools.partial(_nsa_proj_kernel, qw=qw),
        grid=(n // tm,),
        in_specs=[pl.BlockSpec((tm, d), row), pl.BlockSpec(g.shape, lambda i: (0, 0)),
                  pl.BlockSpec(w_in.shape, lambda i: (0, 0)), tspec, tspec, tspec],
        out_specs=[pl.BlockSpec((tm, w), row) for w in widths],
        out_shape=[jax.ShapeDtypeStruct((n, w), F32) for w in widths],
        compiler_params=_cp("parallel"),
        name="nsa_proj",
    )(x, g, w_in, *tabs)


def _compress_kernel(xk_ref, xv_ref, pe_ref, wk_ref, wv_ref, o_ref, *, mb):
    acck = jnp.zeros((mb, LANES), F32)
    accv = jnp.zeros((mb, LANES), F32)
    for l in range(BLK):
        ak = xk_ref[pl.ds(l, mb, stride=BLK), :] + pe_ref[l:l + 1, 0:LANES]
        av = xv_ref[pl.ds(l, mb, stride=BLK), :] + pe_ref[l:l + 1, LANES:2 * LANES]
        acck = acck + _dot(ak.astype(BF16), wk_ref[l])
        accv = accv + _dot(av.astype(BF16), wv_ref[l])
    o_ref[:, 0:LANES] = acck
    o_ref[:, LANES:2 * LANES] = accv


def _compress(rows2d, pe2, wk, wv):
    nblk = rows2d.shape[0] // BLK
    mb = _tile(nblk, 128)
    return pl.pallas_call(
        functools.partial(_compress_kernel, mb=mb),
        grid=(nblk // mb,),
        in_specs=[pl.BlockSpec((mb * BLK, LANES), lambda i: (i, 0)),
                  pl.BlockSpec((mb * BLK, LANES), lambda i: (i, 1)),
                  pl.BlockSpec(pe2.shape, lambda i: (0, 0)),
                  pl.BlockSpec(wk.shape, lambda i: (0, 0, 0)),
                  pl.BlockSpec(wv.shape, lambda i: (0, 0, 0))],
        out_specs=pl.BlockSpec((mb, 2 * LANES), lambda i: (i, 0)),
        out_shape=jax.ShapeDtypeStruct((nblk, 2 * LANES), F32),
        compiler_params=_cp("parallel"),
        name="nsa_compress",
    )(rows2d, rows2d, pe2, wk, wv)


def _compress_t_kernel(x_ref, pe_ref, w_ref, o_ref, *, pp):
    for c in range(2):
        acc = jnp.zeros((2 * pp, o_ref.shape[2]), F32)
        for d in range(HEAD_DIM):
            a = jnp.concatenate([x_ref[:, c * LANES + d, :], x_ref[:, c * LANES + HEAD_DIM + d, :]], axis=0)
            a = a + pe_ref[c * HEAD_DIM + d:c * HEAD_DIM + d + 1, :]
            acc = acc + _dot(a.astype(BF16), w_ref[c, d])
        o_ref[2 * c] = acc[0:pp]
        o_ref[2 * c + 1] = acc[pp:2 * pp]


def _compress_pool(cache_t, pe, phi):
    n_pool, _, _, page = cache_t.shape
    bpp = page // BLK
    x = cache_t.reshape(n_pool, 4 * LANES, page)
    pe_t = jnp.tile(jnp.transpose(pe, (0, 2, 1)), (1, 1, bpp)).reshape(2 * HEAD_DIM, page)
    ph = jnp.transpose(phi.reshape(2, BLK, HEAD_DIM, HEAD_DIM), (0, 2, 1, 3))
    eye = jnp.eye(bpp, dtype=F32)
    w = jnp.einsum('ab,cdle->cdalbe', eye, ph).reshape(2, HEAD_DIM, page, bpp * HEAD_DIM).astype(BF16)
    pp = _tile(n_pool, 128)
    out = pl.pallas_call(
        functools.partial(_compress_t_kernel, pp=pp),
        grid=(n_pool // pp,),
        in_specs=[pl.BlockSpec((pp, 2 * LANES, page), lambda i: (i, 0, 0)),
                  pl.BlockSpec(pe_t.shape, lambda i: (0, 0)),
                  pl.BlockSpec(w.shape, lambda i: (0, 0, 0, 0))],
        out_specs=pl.BlockSpec((4, pp, bpp * HEAD_DIM), lambda i: (0, i, 0)),
        out_shape=jax.ShapeDtypeStruct((4, n_pool, bpp * HEAD_DIM), F32),
        compiler_params=_cp("parallel"),
        name="nsa_compress_pool",
    )(x, pe_t, w)
    out = out.reshape(2, 2, n_pool, bpp, HEAD_DIM).transpose(2, 3, 0, 1, 4)
    return out.reshape(n_pool, bpp, 4 * HEAD_DIM)


def _compress_weights(pe, phi):
    pe2 = jnp.concatenate([pe[0], pe[0], pe[1], pe[1]], axis=1)
    ph = phi.reshape(2, BLK, HEAD_DIM, HEAD_DIM)
    z = jnp.zeros_like(ph[0])
    def bd(p):
        return jnp.concatenate([jnp.concatenate([p, z], axis=2), jnp.concatenate([z, p], axis=2)], axis=1).astype(BF16)
    return pe2, bd(ph[0]), bd(ph[1])


def _select_blocks(imp, nb):
    lane = lax.broadcasted_iota(jnp.int32, imp.shape, 1)
    cnt = jnp.zeros(imp.shape, F32)
    for i in range(nb):
        ci = imp[:, i:i + 1]
        ahead = (ci > imp) | ((ci == imp) & (lane > i))
        cnt = cnt + jnp.where(ahead, 1.0, 0.0)
    return jnp.where(cnt < float(min(TOPK, nb)), 1.0, 0.0)


def _masked_softmax(s, mask):
    s = jnp.where(mask, s, NEG)
    m = jnp.max(s, axis=-1, keepdims=True)
    e = jnp.where(mask, jnp.exp(s - m), 0.0)
    den = jnp.sum(e, axis=-1, keepdims=True)
    return e / jnp.where(den > 0, den, 1.0)


def _place(x, g):
    z = jnp.zeros_like(x)
    return jnp.concatenate([x, z] if g == 0 else [z, x], axis=1)


def _nsa_prompt_kernel(qq_ref, gate_ref, cmp_ref, e_ref, sv_ref, wv_ref, o_ref, bias_scr,
                       *, tq, kc, nb, nheads, wk):
    qi = pl.program_id(1)
    qw = nheads * HEAD_DIM
    hg = nheads // 2
    t = qi * tq + lax.broadcasted_iota(jnp.int32, (tq, 1), 0)
    qq = qq_ref[0]
    gates = gate_ref[0]
    cmp = cmp_ref[0]
    ck = cmp[:, 0:LANES].astype(BF16)
    cv = cmp[:, LANES:2 * LANES].astype(BF16)
    blk = lax.broadcasted_iota(jnp.int32, (tq, nb), 1)
    done = (blk + 1) * BLK - 1 <= t
    cur = t // BLK
    forced = (blk == cur) | (blk == 0)
    future = blk > cur
    nchunks = (qi * tq + tq - 1) // kc + 1
    wstart = pl.multiple_of(jnp.maximum(qi * tq - WINDOW, 0), tq)
    kwin = wv_ref[0, pl.ds(wstart, wk), 0:LANES].astype(BF16)
    vwin = wv_ref[0, pl.ds(wstart, wk), LANES:2 * LANES].astype(BF16)
    wpos = wstart + lax.broadcasted_iota(jnp.int32, (tq, wk), 1)
    wmask = (wpos <= t) & (wpos > t - WINDOW)
    kpos_c = lax.broadcasted_iota(jnp.int32, (tq, kc), 1)

    rows = hg * tq
    outs = []
    for g in range(2):
        def stack(off):
            return jnp.concatenate(
                [_place(qq[:, off + (g * hg + h8) * HEAD_DIM:off + (g * hg + h8 + 1) * HEAD_DIM], g)
                 for h8 in range(hg)], axis=0).astype(BF16)

        q = stack(0)
        qr = stack(qw)
        gcols = jnp.concatenate([gates[:, 3 * (g * hg + h8):3 * (g * hg + h8) + 3] for h8 in range(hg)], axis=0)

        s = (_dot_nt(q, ck) * QK_SCALE).reshape(hg, tq, nb)
        p = _masked_softmax(s, done[None])
        imp = jnp.sum(p, axis=0)
        o_cmp = _dot(p.reshape(rows, nb).astype(BF16), cv)
        imp = jnp.where(forced, FORCE, jnp.where(future, -1.0, imp))
        sel = _select_blocks(imp, nb).astype(BF16)

        def fill(c, _):
            allowed = (_dot(sel, e_ref[c]) > 0.5) & (c * kc + kpos_c <= t)
            bias_scr[c] = jnp.where(allowed, 0.0, NEG)
            return 0
        lax.fori_loop(0, nchunks, fill, 0)

        def body(c, carry):
            m, l, acc = carry
            start = pl.multiple_of(c * kc, kc)
            ks = sv_ref[0, pl.ds(start, kc), 0:LANES].astype(BF16)
            vs = sv_ref[0, pl.ds(start, kc), LANES:2 * LANES].astype(BF16)
            s = (_dot_nt(qr, ks) * QK_SCALE).reshape(hg, tq, kc) + bias_scr[c][None]
            s = s.reshape(rows, kc)
            m_new = jnp.maximum(m, jnp.max(s, axis=1, keepdims=True))
            alpha = jnp.exp(m - m_new)
            e = jnp.exp(s - m_new)
            l = alpha * l + jnp.sum(e, axis=1, keepdims=True)
            acc = alpha * acc + _dot(e.astype(BF16), vs)
            return m_new, l, acc

        _, l, acc = lax.fori_loop(
            0, nchunks, body,
            (jnp.full((rows, 1), NEG, F32), jnp.zeros((rows, 1), F32), jnp.zeros((rows, LANES), F32)))
        o_slc = acc / l

        sw = (_dot_nt(qr, kwin) * QK_SCALE).reshape(hg, tq, wk)
        pw = _masked_softmax(sw, wmask[None]).reshape(rows, wk)
        o_win = _dot(pw.astype(BF16), vwin)

        o = gcols[:, 0:1] * o_cmp + gcols[:, 1:2] * o_slc + gcols[:, 2:3] * o_win
        for h8 in range(hg):
            outs.append(o[h8 * tq:(h8 + 1) * tq, g * HEAD_DIM:(g + 1) * HEAD_DIM])
    o_ref[0] = jnp.concatenate(outs, axis=1)


def _block_expand(nb_rows, nkeys, kc):
    key = jnp.arange(nkeys).reshape(nkeys // kc, 1, kc)
    return (key // BLK == jnp.arange(nb_rows)[None, :, None]).astype(BF16)


def _nsa_prompt_attn(qq, gates, cmp, prow, wrow, nheads):
    b, t, _ = qq.shape
    qw = nheads * HEAD_DIM
    tq = _tile(t, 128)
    kc = _tile(t, 512)
    nb = t // BLK
    wk = min(t, tq + WINDOW)
    e = _block_expand(nb, t, kc)
    kern = functools.partial(_nsa_prompt_kernel, tq=tq, kc=kc, nb=nb, nheads=nheads, wk=wk)
    return pl.pallas_call(
        kern,
        grid=(b, t // tq),
        in_specs=[
            pl.BlockSpec((1, tq, 2 * qw), lambda b_, i: (b_, i, 0)),
            pl.BlockSpec((1, tq, LANES), lambda b_, i: (b_, i, 0)),
            pl.BlockSpec((1, nb, 2 * LANES), lambda b_, i: (b_, 0, 0)),
            pl.BlockSpec(e.shape, lambda b_, i: (0, 0, 0)),
            pl.BlockSpec((1, t, 2 * LANES), lambda b_, i: (b_, 0, 1)),
            pl.BlockSpec((1, t, 2 * LANES), lambda b_, i: (b_, 0, 0)),
        ],
        out_specs=pl.BlockSpec((1, tq, qw), lambda b_, i: (b_, i, 0)),
        out_shape=jax.ShapeDtypeStruct((b, t, qw), F32),
        scratch_shapes=[pltpu.VMEM((t // kc, tq, kc), F32)],
        compiler_params=_cp("parallel", "arbitrary"),
        name="nsa_prompt_attn",
    )(qq, gates, cmp, e, prow, wrow)


def _nsa_sample_kernel(pt_ref, q_ref, qr_ref, gate_ref, cnew_ref, e_ref, new_ref, win_ref, wnew_ref,
                       *rest, n_pages, page, nq, hg, nbp, past):
    pages = rest[:n_pages]
    cmps = rest[n_pages:2 * n_pages]
    o_ref = rest[2 * n_pages]
    cmp_scr = rest[2 * n_pages + 1]
    rows = 2 * nq * hg
    bpp = page // BLK
    nb = n_pages * bpp + 1
    nnew = new_ref.shape[1]
    nwin = win_ref.shape[3]

    cmp_scr[...] = jnp.zeros_like(cmp_scr)
    for p in range(n_pages):
        cmp_scr[p * bpp:(p + 1) * bpp, :] = cmps[p][0]
    cmp_scr[n_pages * bpp:n_pages * bpp + 1, :] = cnew_ref[0, 0:1, :]
    cmp = cmp_scr[...]
    ck = cmp[:, 0:LANES].astype(BF16)
    cv = cmp[:, LANES:2 * LANES].astype(BF16)

    q = q_ref[0].astype(BF16)
    qr = qr_ref[0].astype(BF16)
    gates = gate_ref[0]
    rowi = lax.broadcasted_iota(jnp.int32, (rows, 1), 0)
    t = past + (rowi // hg) % nq
    top = rowi < rows // 2

    def pick(x):
        return jnp.where(top, x[:, 0:HEAD_DIM], x[:, HEAD_DIM:2 * HEAD_DIM])

    blk = lax.broadcasted_iota(jnp.int32, (rows, nbp), 1)
    done = ((blk + 1) * BLK - 1 <= t) & (blk < nb)
    p_cmp = _masked_softmax(_dot_nt(q, ck) * QK_SCALE, done)
    o_cmp = pick(_dot(p_cmp.astype(BF16), cv))

    ng = rows // hg
    imp = jnp.sum(p_cmp.reshape(ng, hg, nbp), axis=1)
    tg = past + lax.broadcasted_iota(jnp.int32, (ng, 1), 0) % nq
    blk_g = lax.broadcasted_iota(jnp.int32, (ng, nbp), 1)
    cur = tg // BLK
    imp = jnp.where((blk_g == cur) | (blk_g == 0), FORCE, jnp.where(blk_g > cur, -1.0, imp))
    sel = _select_blocks(imp, nb)
    expand = (lax.broadcasted_iota(jnp.int32, (rows, ng), 0) // hg
              == lax.broadcasted_iota(jnp.int32, (rows, ng), 1)).astype(F32)
    sel_f = _dot(expand, sel)
    sel_rows = sel_f.astype(BF16)

    allowed = _dot(sel_rows, e_ref[...]) > 0.5
    s_parts = [_dot(qr, pages[p][0, 0].astype(BF16)) for p in range(n_pages)]
    s = jnp.concatenate(s_parts, axis=1) * QK_SCALE
    s = jnp.where(allowed, s, NEG)
    new = new_ref[0]
    s_n = _dot_nt(qr, new[:, 0:LANES].astype(BF16)) * QK_SCALE
    npos = past + lax.broadcasted_iota(jnp.int32, (rows, nnew), 1)
    mask_n = (npos <= t) & (sel_f[:, nb - 1:nb] > 0.5)
    s_n = jnp.where(mask_n, s_n, NEG)
    m = jnp.maximum(jnp.max(s, axis=1, keepdims=True), jnp.max(s_n, axis=1, keepdims=True))
    e = jnp.where(allowed, jnp.exp(s - m), 0.0)
    e_n = jnp.where(mask_n, jnp.exp(s_n - m), 0.0)
    den = jnp.sum(e, axis=1, keepdims=True) + jnp.sum(e_n, axis=1, keepdims=True)
    acc = _dot(e_n.astype(BF16), new[:, LANES:2 * LANES].astype(BF16))
    for p in range(n_pages):
        acc = acc + _dot_nt(e[:, p * page:(p + 1) * page].astype(BF16), pages[p][0, 1].astype(BF16))
    o_slc = pick(acc / jnp.where(den > 0, den, 1.0))

    wnew = wnew_ref[0]
    s_w = _dot(qr, win_ref[0, 0].astype(BF16)) * QK_SCALE
    wpos = past - nwin + lax.broadcasted_iota(jnp.int32, (rows, nwin), 1)
    mask_w = (wpos <= t) & (wpos > t - WINDOW) & (wpos >= 0)
    s_w = jnp.where(mask_w, s_w, NEG)
    s_wn = _dot_nt(qr, wnew[:, 0:LANES].astype(BF16)) * QK_SCALE
    mask_wn = (npos <= t) & (npos > t - WINDOW)
    s_wn = jnp.where(mask_wn, s_wn, NEG)
    m = jnp.maximum(jnp.max(s_w, axis=1, keepdims=True), jnp.max(s_wn, axis=1, keepdims=True))
    e = jnp.where(mask_w, jnp.exp(s_w - m), 0.0)
    e_n = jnp.where(mask_wn, jnp.exp(s_wn - m), 0.0)
    den = jnp.sum(e, axis=1, keepdims=True) + jnp.sum(e_n, axis=1, keepdims=True)
    acc = _dot_nt(e.astype(BF16), win_ref[0, 1].astype(BF16)) \
        + _dot(e_n.astype(BF16), wnew[:, LANES:2 * LANES].astype(BF16))
    o_win = pick(acc / jnp.where(den > 0, den, 1.0))

    o_ref[0] = gates[:, 0:1] * o_cmp + gates[:, 1:2] * o_slc + gates[:, 2:3] * o_win


def _nsa_sample_attn(q, q_rot, gates, cmp_pool, cmp_new, cache, new_blk, win_state, wnew, page_table, nq):
    db, rows, _ = q.shape
    n_pages = page_table.shape[1]
    page = cache.shape[3]
    past = n_pages * page
    nb = n_pages * (page // BLK) + 1
    nbp = -(-nb // 8) * 8
    nkeys = n_pages * page
    e = (jnp.arange(nkeys)[None, :] // BLK == jnp.arange(nbp)[:, None]).astype(BF16)
    nnew = new_blk.shape[1]
    kern = functools.partial(_nsa_sample_kernel, n_pages=n_pages, page=page, nq=nq, hg=rows // (2 * nq),
                             nbp=nbp, past=past)

    bpp = page // BLK
    cmp_pool3 = cmp_pool

    def page_spec(p):
        return pl.BlockSpec((1, 2, LANES, page), lambda b, pt: (pt[b, p], 1, 0, 0))

    def cmp_spec(p):
        return pl.BlockSpec((1, bpp, 2 * LANES), lambda b, pt: (pt[b, p], 0, 0))

    grid_spec = pltpu.PrefetchScalarGridSpec(
        num_scalar_prefetch=1,
        grid=(db,),
        in_specs=[
            pl.BlockSpec((1, rows, LANES), lambda b, pt: (b, 0, 0)),
            pl.BlockSpec((1, rows, LANES), lambda b, pt: (b, 0, 0)),
            pl.BlockSpec((1, rows, 3), lambda b, pt: (b, 0, 0)),
            pl.BlockSpec((1, 1, 2 * LANES), lambda b, pt: (b, 0, 0)),
            pl.BlockSpec(e.shape, lambda b, pt: (0, 0)),
            pl.BlockSpec((1, nnew, 2 * LANES), lambda b, pt: (b, 0, 1)),
            pl.BlockSpec((1,) + win_state.shape[1:], lambda b, pt: (b, 0, 0, 0)),
            pl.BlockSpec((1, nnew, 2 * LANES), lambda b, pt: (b, 0, 0)),
        ] + [page_spec(p) for p in range(n_pages)] + [cmp_spec(p) for p in range(n_pages)],
        out_specs=pl.BlockSpec((1, rows, HEAD_DIM), lambda b, pt: (b, 0, 0)),
        scratch_shapes=[pltpu.VMEM((nbp, 2 * LANES), F32)],
    )
    return pl.pallas_call(
        kern,
        grid_spec=grid_spec,
        out_shape=jax.ShapeDtypeStruct((db, rows, HEAD_DIM), F32),
        compiler_params=_cp("arbitrary"),
        name="nsa_sample_attn",
    )(page_table, q, q_rot, gates, cmp_new, e, new_blk, win_state, wnew,
      *([cache] * n_pages), *([cmp_pool3] * n_pages))


def kernel(x_prompt, x_sample, mem_prompt, page_table, cache_kv_l0, cache_mem_l0, cache_nsa_l1, state_win_l1,
           cache_mem_l1, cache_kv_l2, cache_mem_l2, cache_nsa_l3, state_win_l3, cache_mem_l3, norm_g, mem_norm_g,
           ffn_w_in, ffn_w_out, w_q_mem, w_kv_mem, w_o_mem, ab_w_in, ab_w_out, diff_lambda, diff_subln_g,
           nsa_w_in, nsa_w_out, nsa_cmp_pe, nsa_cmp_phi):
    bsz, seq, d = x_prompt.shape
    db, dq, _ = x_sample.shape
    depth = norm_g.shape[0]
    n_pages = page_table.shape[1]
    page = cache_kv_l0.shape[1]
    past = n_pages * page
    mlen = mem_prompt.shape[1]
    mem_w = w_q_mem.shape[2]
    mem_heads = mem_w // LANES
    nheads = d // HEAD_DIM
    qw = nheads * HEAD_DIM
    kv_caches = (cache_kv_l0, cache_kv_l2)
    nsa_caches = (cache_nsa_l1, cache_nsa_l3)
    win_states = (state_win_l1, state_win_l3)
    mem_caches = (cache_mem_l0, cache_mem_l1, cache_mem_l2, cache_mem_l3)

    np_, ns_ = bsz * seq, db * dq
    tm_p = _tile(seq, 512)
    tm_s = ns_
    tabs_p = _rope_tables(jnp.arange(seq, dtype=jnp.int32))
    tabs_s = _rope_tables(past + jnp.arange(ns_, dtype=jnp.int32) % dq)
    spt = seq // tm_p
    map_p = lambda i: (i % spt, 0)
    map_s = lambda i: (0, 0)

    bf = lambda w: w.astype(BF16)
    ffn_in, ffn_out = bf(ffn_w_in), bf(ffn_w_out)
    wq_m, wkv_m, wo_m = bf(w_q_mem), bf(w_kv_mem), bf(w_o_mem)
    ab_in, ab_out = bf(ab_w_in), bf(ab_w_out)
    nsa_in = bf(jnp.pad(nsa_w_in, ((0, 0), (0, 0), (0, qw + 7 * LANES - nsa_w_in.shape[2]))))
    nsa_out = bf(nsa_w_out)

    mkv_all = _mem_kv(mem_prompt.reshape(bsz * mlen, d), mem_norm_g.reshape(depth, 1, d), wkv_m)

    xp = x_prompt.reshape(np_, d)
    xs = x_sample.reshape(ns_, d)
    kv_p, kv_s, nsa_p, nsa_s, win_p, win_s, mem_p = [], [], [], [], [], [], []
    om_p = om_s = None
    for layer in range(depth):
        g = norm_g[layer]
        i = layer // 2
        if layer == 0:
            xp = _ffn_half(xp, g, ffn_in[layer, 0], ffn_out[layer, 0], 0, 1)
            xs = _ffn_half(xs, g, ffn_in[layer, 0], ffn_out[layer, 0], 0, 1)
        if layer % 2 == 0:
            lam_init = 0.8 - 0.6 * math.exp(-0.3 * layer)
            lam = diff_lambda[i]
            sg = diff_subln_g[i]
            qz_p, kvr_p = _ab_proj(xp, g, ab_in[i], tabs_p, map_p, tm_p)
            qz_s, kvr_s = _ab_proj(xs, g, ab_in[i], tabs_s, map_s, tm_s)
            kvw = kvr_p.shape[1]
            mp = _ab_prompt_attn(qz_p.reshape(bsz, seq, -1), kvr_p.reshape(bsz, seq, kvw), lam, sg, lam_init)
            ms = _ab_sample_attn(qz_s.reshape(db, dq, -1), kvr_s.reshape(db, dq, kvw), kv_caches[i], page_table,
                                 lam, sg, lam_init, npg=_tile(n_pages, 8))
            kv_p.append(kvr_p.reshape(bsz, seq, 2, kvw // (2 * HEAD_DIM), HEAD_DIM))
            kv_s.append(kvr_s.reshape(db, dq, 2, kvw // (2 * HEAD_DIM), HEAD_DIM))
            mp = mp.reshape(np_, -1)
            ms = ms.reshape(ns_, -1)
            w_mix = ab_out[i]
        else:
            qq_p, prow_p, wrow_p, gate_p = _nsa_proj(xp, g, nsa_in[i], qw, tabs_p, map_p, tm_p)
            qq_s, prow_s, wrow_s, gate_s = _nsa_proj(xs, g, nsa_in[i], qw, tabs_s, map_s, tm_s)
            pe2, wk, wv = _compress_weights(nsa_cmp_pe[i], nsa_cmp_phi[i])
            cmp_p = _compress(prow_p, pe2, wk, wv).reshape(bsz, seq // BLK, 2 * LANES)
            mp = _nsa_prompt_attn(qq_p.reshape(bsz, seq, -1), gate_p.reshape(bsz, seq, LANES), cmp_p,
                                  prow_p.reshape(bsz, seq, -1), wrow_p.reshape(bsz, seq, -1), nheads)
            mp = mp.reshape(np_, -1)
            cache = nsa_caches[i]
            n_pool = cache.shape[0]
            roww = prow_s.shape[1]
            cache_t = jnp.transpose(cache, (0, 2, 3, 4, 1)).reshape(n_pool, 4, LANES, page)
            cmp_pool = _compress_pool(cache_t, nsa_cmp_pe[i], nsa_cmp_phi[i])
            new_blk = jnp.pad(prow_s.reshape(db, dq, roww), ((0, 0), (0, BLK - dq), (0, 0)))
            cmp_new = _compress(new_blk.reshape(db * BLK, roww), pe2, wk, wv).reshape(db, 1, 2 * LANES)
            wnew = jnp.pad(wrow_s.reshape(db, dq, -1), ((0, 0), (0, BLK - dq), (0, 0)))
            nwin = win_states[i].shape[1]
            win_state = jnp.transpose(win_states[i], (0, 2, 3, 4, 1)).reshape(db, 2, LANES, nwin)
            hg = nheads // 2

            def to_rows(a, pad):
                w = a.shape[1] // nheads
                a = a.reshape(db, dq, 2, hg, w).transpose(0, 2, 1, 3, 4)
                if pad:
                    z = jnp.zeros_like(a[:, 0:1])
                    a = jnp.concatenate([jnp.concatenate([a[:, 0:1], z], axis=-1),
                                         jnp.concatenate([z, a[:, 1:2]], axis=-1)], axis=1)
                return a.reshape(db, 2 * dq * hg, -1)

            q_rows = to_rows(qq_s[:, 0:qw], True)
            qr_rows = to_rows(qq_s[:, qw:2 * qw], True)
            g_rows = to_rows(gate_s[:, 0:3 * nheads], False)
            o_rows = _nsa_sample_attn(q_rows, qr_rows, g_rows, cmp_pool, cmp_new, cache_t, new_blk, win_state, wnew,
                                      page_table, dq)
            ms = o_rows.reshape(db, 2, dq, hg, HEAD_DIM).transpose(0, 2, 1, 3, 4).reshape(ns_, qw)
            nsa_p.append(prow_p.reshape(bsz, seq, 4, 2, HEAD_DIM))
            nsa_s.append(prow_s.reshape(db, dq, 4, 2, HEAD_DIM))
            n_keep = min(WINDOW, seq)
            win_p.append(wrow_p.reshape(bsz, seq, 2, 2, HEAD_DIM)[:, seq - n_keep:])
            win_all = jnp.concatenate([win_states[i], wrow_s.reshape(db, dq, 2, 2, HEAD_DIM)], axis=1)
            win_s.append(win_all[:, dq:])
            w_mix = nsa_out[i]

        xp, qm_p = _mix_out(mp, xp, g, w_mix, wq_m[layer])
        xs, qm_s = _mix_out(ms, xs, g, w_mix, wq_m[layer])
        mkv = mkv_all[layer].reshape(bsz, mlen, 2 * mem_w)
        om_p = _mem_attn(qm_p.reshape(bsz, seq, mem_w), mkv, mem_heads).reshape(np_, mem_w)
        om_s = _mem_attn(qm_s.reshape(db, dq, mem_w), mem_caches[layer].reshape(db, mlen * 2 * mem_heads, LANES),
                         mem_heads, interleaved=True).reshape(ns_, mem_w)
        mem_p.append(mkv.reshape(bsz, mlen, 2, mem_heads, LANES))
        xp = _ffn_half(xp, g, ffn_in[layer, 1], ffn_out[layer, 1], 6, 7, om_p, wo_m[layer], 5)
        xs = _ffn_half(xs, g, ffn_in[layer, 1], ffn_out[layer, 1], 6, 7, om_s, wo_m[layer], 5)
        if layer + 1 < depth:
            gn = norm_g[layer + 1]
            xp = _ffn_half(xp, gn, ffn_in[layer + 1, 0], ffn_out[layer + 1, 0], 0, 1)
            xs = _ffn_half(xs, gn, ffn_in[layer + 1, 0], ffn_out[layer + 1, 0], 0, 1)

    return (xp.reshape(bsz, seq, d), xs.reshape(db, dq, d),
            kv_p[0], kv_s[0], mem_p[0],
            nsa_p[0], nsa_s[0], win_p[0], win_s[0], mem_p[1],
            kv_p[1], kv_s[1], mem_p[2],
            nsa_p[1], nsa_s[1], win_p[1], win_s[1], mem_p[3])
```

```python
import functools
import math

import jax
import jax.numpy as jnp
from jax import lax
from jax.experimental import pallas as pl
from jax.experimental.pallas import tpu as pltpu

F32 = jnp.float32
BF16 = jnp.bfloat16

HEAD_DIM = 64
ROT_DIM = HEAD_DIM // 4
ROPE_THETA = 500000.0
BLK = 64
TOPK = 8
WINDOW = 256
EPS = 1e-6
NEG = -1e30
FORCE = 1e4
LANES = 128
QK_SCALE = 1.0 / math.sqrt(HEAD_DIM)

VMEM_LIMIT = 56 * 1024 * 1024


def _cp(*sem):
    return pltpu.CompilerParams(dimension_semantics=sem, vmem_limit_bytes=VMEM_LIMIT)


def _rms(x, g):
    return x * lax.rsqrt(jnp.mean(x * x, axis=-1, keepdims=True) + EPS) * g


def _dot(a, b):
    return jnp.dot(a, b, preferred_element_type=F32)


def _dot_nt(a, b):
    return lax.dot_general(a, b, (((1,), (1,)), ((), ())), preferred_element_type=F32)


def _softplus(z):
    return jnp.maximum(z, 0.0) + jnp.log(1.0 + jnp.exp(-jnp.abs(z)))


def _split_bf16(x):
    hi = x.astype(BF16)
    lo = (x - hi.astype(F32)).astype(BF16)
    return hi, lo


def _tile(n, pref):
    t = min(n, pref)
    while n % t:
        t //= 2
    return t


def _rope_tables(pos):
    half = ROT_DIM // 2
    inv = jnp.power(ROPE_THETA, -jnp.arange(half, dtype=F32) * 2.0 / ROT_DIM)
    ang = pos.astype(F32)[:, None] * inv[None, :]
    cos = jnp.cos(ang)
    sin = jnp.sin(ang)
    p = pos.shape[0]
    ones = jnp.ones((p, HEAD_DIM - ROT_DIM), F32)
    zeros = jnp.zeros((p, HEAD_DIM - ROT_DIM), F32)
    zh = jnp.zeros((p, half), F32)
    c = jnp.concatenate([cos, cos, ones], axis=1)
    s1 = jnp.concatenate([-sin, zh, zeros], axis=1)
    s2 = jnp.concatenate([zh, sin, zeros], axis=1)
    rep = LANES // HEAD_DIM
    return jnp.tile(c, (1, rep)), jnp.tile(s1, (1, rep)), jnp.tile(s2, (1, rep))


def _rope_cols(z, c, s1, s2):
    outs = []
    for i in range(z.shape[1] // LANES):
        xb = z[:, i * LANES:(i + 1) * LANES]
        outs.append(xb * c + pltpu.roll(xb, LANES - ROT_DIM // 2, 1) * s1 + pltpu.roll(xb, ROT_DIM // 2, 1) * s2)
    return outs[0] if len(outs) == 1 else jnp.concatenate(outs, axis=1)


def _ffn_kernel(*refs, nj, pre, ipre, ipost, imem):
    if pre:
        x_ref, om_ref, wom_ref, g_ref, wg_ref, wu_ref, wo_ref, o_ref, h_scr, acc_scr, x_scr = refs
    else:
        x_ref, g_ref, wg_ref, wu_ref, wo_ref, o_ref, h_scr, acc_scr, x_scr = refs
    j = pl.program_id(1)

    @pl.when(j == 0)
    def _():
        x = x_ref[...]
        if pre:
            y = _dot(om_ref[...].astype(BF16), wom_ref[...])
            x = x + _rms(y, g_ref[imem:imem + 1, :])
        x_scr[...] = x
        h_scr[...] = _rms(x, g_ref[ipre:ipre + 1, :]).astype(BF16)
        acc_scr[...] = jnp.zeros_like(acc_scr)

    h = h_scr[...]
    gate = _dot(h, wg_ref[...])
    up = _dot(h, wu_ref[...])
    act = (gate * jax.nn.sigmoid(gate)) * up
    acc_scr[...] += _dot(act.astype(BF16), wo_ref[...])

    @pl.when(j == nj - 1)
    def _():
        o_ref[...] = x_scr[...] + 0.5 * _rms(acc_scr[...], g_ref[ipost:ipost + 1, :])


def _ffn_half(x, g, w_in, w_out, ipre, ipost, om=None, w_om=None, imem=None):
    n, d = x.shape
    dff = w_out.shape[0]
    tm = _tile(n, 512)
    tf = dff // 2 if (dff // 2) % LANES == 0 else dff
    nj = dff // tf
    pre = om is not None
    kern = functools.partial(_ffn_kernel, nj=nj, pre=pre, ipre=ipre, ipost=ipost, imem=imem)
    row = lambda i, j: (i, 0)
    in_specs = [pl.BlockSpec((tm, d), row)]
    args = [x]
    if pre:
        in_specs += [pl.BlockSpec((tm, om.shape[1]), row), pl.BlockSpec(w_om.shape, lambda i, j: (0, 0))]
        args += [om, w_om]
    in_specs += [
        pl.BlockSpec(g.shape, lambda i, j: (0, 0)),
        pl.BlockSpec((d, tf), lambda i, j: (0, j)),
        pl.BlockSpec((d, tf), lambda i, j: (0, nj + j)),
        pl.BlockSpec((tf, d), lambda i, j: (j, 0)),
    ]
    args += [g, w_in, w_in, w_out]
    return pl.pallas_call(
        kern,
        grid=(n // tm, nj),
        in_specs=in_specs,
        out_specs=pl.BlockSpec((tm, d), row),
        out_shape=jax.ShapeDtypeStruct((n, d), F32),
        scratch_shapes=[pltpu.VMEM((tm, d), BF16), pltpu.VMEM((tm, d), F32), pltpu.VMEM((tm, d), F32)],
        compiler_params=_cp("parallel", "arbitrary"),
        name="ffn_half",
    )(*args)


def _ab_proj_kernel(x_ref, g_ref, w_ref, c_ref, s1_ref, s2_ref, qz_ref, kv_ref, *, sbw):
    h = _rms(x_ref[...], g_ref[2:3, :]).astype(BF16)
    c, s1, s2 = c_ref[...], s1_ref[...], s2_ref[...]

    def proj(i):
        return _dot(h, w_ref[:, i * sbw:(i + 1) * sbw])

    qz_ref[:, 0:sbw] = proj(0)
    kv_ref[:, 0:sbw] = proj(1)
    kv_ref[:, 2 * sbw:3 * sbw] = proj(2)
    qz_ref[:, sbw:2 * sbw] = _rope_cols(proj(3), c, s1, s2)
    kv_ref[:, sbw:2 * sbw] = _rope_cols(proj(4), c, s1, s2)
    kv_ref[:, 3 * sbw:4 * sbw] = proj(5)


def _ab_proj(x, g, w_in, tabs, tab_map, tm):
    n, d = x.shape
    sbw = w_in.shape[1] // 6
    row = lambda i: (i, 0)
    tspec = pl.BlockSpec((tm, LANES), tab_map)
    return pl.pallas_call(
        functools.partial(_ab_proj_kernel, sbw=sbw),
        grid=(n // tm,),
        in_specs=[pl.BlockSpec((tm, d), row), pl.BlockSpec(g.shape, lambda i: (0, 0)),
                  pl.BlockSpec(w_in.shape, lambda i: (0, 0)), tspec, tspec, tspec],
        out_specs=[pl.BlockSpec((tm, 2 * sbw), row), pl.BlockSpec((tm, 4 * sbw), row)],
        out_shape=[jax.ShapeDtypeStruct((n, 2 * sbw), F32), jax.ShapeDtypeStruct((n, 4 * sbw), F32)],
        compiler_params=_cp("parallel"),
        name="ab_proj",
    )(x, g, w_in, *tabs)


def _diff_lambda(lam_ref, lam_init):
    lv = lam_ref[...]
    a = jnp.sum(lv[0:1, :] * lv[1:2, :], axis=1, keepdims=True)
    b = jnp.sum(lv[2:3, :] * lv[3:4, :], axis=1, keepdims=True)
    return jnp.exp(a) - jnp.exp(b) + lam_init


def _ab_prompt_kernel(lam_ref, sg_ref, uu_ref, q_ref, k_ref, v_ref, o_ref, *, tq, rs, n_sb, lam_init):
    u = pl.program_id(1)
    qi = pl.program_id(2)
    q = q_ref[0]
    lane_half = lax.broadcasted_iota(jnp.int32, (tq, LANES), 1) // HEAD_DIM
    qsc = q * QK_SCALE
    qs = jnp.concatenate([jnp.where(lane_half == 0, qsc, 0.0), jnp.where(lane_half == 1, qsc, 0.0)],
                         axis=0).astype(BF16)
    nst = 2 * tq // rs
    strips = [qs[i * rs:(i + 1) * rs] for i in range(nst)]
    row = lax.broadcasted_iota(jnp.int32, (rs, tq), 0)
    col = lax.broadcasted_iota(jnp.int32, (rs, tq), 1)

    def load_kv(kb):
        start = pl.multiple_of(kb * tq, tq)
        return k_ref[0, pl.ds(start, tq), :].astype(BF16), v_ref[0, pl.ds(start, tq), :].astype(BF16)

    @pl.when(u < n_sb)
    def _():
        uu = uu_ref[...]

        def step(kb, carry, diagonal):
            runs, accs = carry
            k, v = load_kv(kb)
            new_runs, new_accs = [], []
            for i in range(nst):
                z = _dot_nt(strips[i], k)
                sp = _softplus(z)
                if diagonal:
                    mask = col < (row + i * rs) % tq
                    lk = jnp.where(mask, -sp, 0.0)
                else:
                    lk = -sp
                hi, lo = _split_bf16(lk)
                later = _dot(jnp.concatenate([hi, lo], axis=1), uu) + runs[i]
                w = jnp.exp(z - sp + later)
                if diagonal:
                    w = jnp.where(mask, w, 0.0)
                new_accs.append(accs[i] + _dot(w.astype(BF16), v))
                new_runs.append(runs[i] + jnp.sum(lk, axis=1, keepdims=True))
            return tuple(new_runs), tuple(new_accs)

        init = (tuple(jnp.zeros((rs, 1), F32) for _ in range(nst)),
                tuple(jnp.zeros((rs, LANES), F32) for _ in range(nst)))
        carry = step(qi, init, True)
        _, accs = lax.fori_loop(1, qi + 1, lambda i, c: step(qi - i, c, False), carry)
        acc = jnp.concatenate(accs, axis=0)
        o_ref[0] = jnp.where(lane_half == 0, acc[0:tq], acc[tq:2 * tq])

    @pl.when(u >= n_sb)
    def _():
        lam = _diff_lambda(lam_ref, lam_init)

        def step(kb, carry, diagonal):
            ms, ls, accs = carry
            k, v = load_kv(kb)
            nm, nl, na = [], [], []
            for i in range(nst):
                s = _dot_nt(strips[i], k)
                if diagonal:
                    mask = col <= (row + i * rs) % tq
                    s = jnp.where(mask, s, NEG)
                m_new = jnp.maximum(ms[i], jnp.max(s, axis=1, keepdims=True))
                alpha = jnp.exp(ms[i] - m_new)
                e = jnp.exp(s - m_new)
                if diagonal:
                    e = jnp.where(mask, e, 0.0)
                nm.append(m_new)
                nl.append(alpha * ls[i] + jnp.sum(e, axis=1, keepdims=True))
                na.append(alpha * accs[i] + _dot(e.astype(BF16), v))
            return tuple(nm), tuple(nl), tuple(na)

        init = (tuple(jnp.full((rs, 1), NEG, F32) for _ in range(nst)),
                tuple(jnp.zeros((rs, 1), F32) for _ in range(nst)),
                tuple(jnp.zeros((rs, LANES), F32) for _ in range(nst)))
        carry = step(qi, init, True)
        _, ls, accs = lax.fori_loop(0, qi, lambda kb, c: step(kb, c, False), carry)
        p = jnp.concatenate(accs, axis=0) / jnp.concatenate(ls, axis=0)
        o = p[0:tq] - lam * p[tq:2 * tq]
        o_ref[0] = _rms(o, sg_ref[...]) * (1.0 - lam_init)


def _cumsum_matrix(tk):
    j = jnp.arange(2 * tk)[:, None] % tk
    s = jnp.arange(tk)[None, :]
    return (j > s).astype(BF16)


def _ab_prompt_attn(qz, kv, lam, subln_g, lam_init, rs=None):
    b, t, w = qz.shape
    nu = w // LANES
    tq = _tile(t, 256)
    kern = functools.partial(_ab_prompt_kernel, tq=tq, rs=rs or 2 * tq, n_sb=nu // 2, lam_init=lam_init)
    return pl.pallas_call(
        kern,
        grid=(b, nu, t // tq),
        in_specs=[
            pl.BlockSpec(lam.shape, lambda b_, u, i: (0, 0)),
            pl.BlockSpec((1, LANES), lambda b_, u, i: (0, 0)),
            pl.BlockSpec((2 * tq, tq), lambda b_, u, i: (0, 0)),
            pl.BlockSpec((1, tq, LANES), lambda b_, u, i: (b_, i, u)),
            pl.BlockSpec((1, t, LANES), lambda b_, u, i: (b_, 0, u)),
            pl.BlockSpec((1, t, LANES), lambda b_, u, i: (b_, 0, nu + u)),
        ],
        out_specs=pl.BlockSpec((1, tq, LANES), lambda b_, u, i: (b_, i, u)),
        out_shape=jax.ShapeDtypeStruct((b, t, w), F32),
        compiler_params=_cp("parallel", "parallel", "arbitrary"),
        name="ab_prompt_attn",
    )(lam, subln_g.reshape(1, LANES), _cumsum_matrix(tq), qz, kv, kv)


def _ab_sample_kernel(pt_ref, lam_ref, sg_ref, uu_ref, q_ref, kvn_ref, *rest, npg, page, nq, past, lam_init):
    pages = rest[:npg]
    o_ref = rest[npg]
    qbd_f, qbd, newpage, o_acc, run_s, m_s, l_s = rest[npg + 1:]
    j = pl.program_id(1)
    nj = pl.num_programs(1)
    nu = q_ref.shape[2] // LANES
    rows = nu * 2 * nq
    hr = rows // 2
    kw = nu * LANES

    @pl.when(j == 0)
    def _():
        q = q_ref[0]
        lane_half = lax.broadcasted_iota(jnp.int32, (nq, LANES), 1) // HEAD_DIM
        qbd_f[...] = jnp.zeros_like(qbd_f)
        for u in range(nu):
            qu = q[:, u * LANES:(u + 1) * LANES]
            piece = jnp.concatenate([jnp.where(lane_half == 0, qu, 0.0), jnp.where(lane_half == 1, qu, 0.0)], axis=0)
            qbd_f[u * 2 * nq:(u + 1) * 2 * nq, u * LANES:(u + 1) * LANES] = piece
        qbd[...] = qbd_f[...].astype(BF16)
        newpage[...] = jnp.zeros_like(newpage)
        newpage[0:nq, :] = kvn_ref[0]
        o_acc[...] = jnp.zeros_like(o_acc)
        run_s[...] = jnp.zeros_like(run_s)
        m_s[...] = jnp.full_like(m_s, NEG)
        l_s[...] = jnp.zeros_like(l_s)

    lane = lax.broadcasted_iota(jnp.int32, (hr, page), 1)
    qpos = past + lax.broadcasted_iota(jnp.int32, (hr, page), 0) % nq
    uu = uu_ref[...]

    def process(scores, weighted_values, page_start):
        s = scores(qbd[...]) * QK_SCALE
        kpos = page_start + lane
        z = s[0:hr]
        mask = kpos < qpos
        sp = _softplus(z)
        lk = jnp.where(mask, -sp, 0.0)
        hi, lo = _split_bf16(lk)
        later = _dot(jnp.concatenate([hi, lo], axis=1), uu) + run_s[...]
        w = jnp.where(mask, jnp.exp(z - sp + later), 0.0)
        run_s[...] += jnp.sum(lk, axis=1, keepdims=True)
        sd = s[hr:rows]
        maskd = kpos <= qpos
        sd = jnp.where(maskd, sd, NEG)
        m_old = m_s[...]
        m_new = jnp.maximum(m_old, jnp.max(sd, axis=1, keepdims=True))
        alpha = jnp.exp(m_old - m_new)
        e = jnp.where(maskd, jnp.exp(sd - m_new), 0.0)
        l_s[...] = alpha * l_s[...] + jnp.sum(e, axis=1, keepdims=True)
        m_s[...] = m_new
        pv = weighted_values(jnp.concatenate([w, e], axis=0).astype(BF16))
        o_acc[0:hr, :] += pv[0:hr]
        o_acc[hr:rows, :] = alpha * o_acc[hr:rows, :] + pv[hr:rows]

    @pl.when(j == 0)
    def _():
        process(lambda q: _dot_nt(q, newpage[:, 0:kw].astype(BF16)),
                lambda w: _dot(w, newpage[:, kw:2 * kw].astype(BF16)), past)

    qb = qbd[...]
    s = jnp.concatenate([_dot(qb, pages[p][0, 0].astype(BF16)) for p in range(npg)], axis=1) * QK_SCALE
    z = s[0:hr]
    sp = _softplus(z)
    lk = -sp
    zl = z + lk
    hi, lo = _split_bf16(lk)
    run = run_s[...]
    ws = []
    for p in range(npg):
        cs = slice(p * page, (p + 1) * page)
        later = _dot(jnp.concatenate([hi[:, cs], lo[:, cs]], axis=1), uu) + run
        ws.append(jnp.exp(zl[:, cs] + later))
        run = run + jnp.sum(lk[:, cs], axis=1, keepdims=True)
    run_s[...] = run
    sd = s[hr:rows]
    m_old = m_s[...]
    m_new = jnp.maximum(m_old, jnp.max(sd, axis=1, keepdims=True))
    alpha = jnp.exp(m_old - m_new)
    e = jnp.exp(sd - m_new)
    l_s[...] = alpha * l_s[...] + jnp.sum(e, axis=1, keepdims=True)
    m_s[...] = m_new
    pv = jnp.zeros((rows, kw), F32)
    for p in range(npg):
        we = jnp.concatenate([ws[p], e[:, p * page:(p + 1) * page]], axis=0).astype(BF16)
        pv = pv + _dot_nt(we, pages[p][0, 1].astype(BF16))
    o_acc[0:hr, :] += pv[0:hr]
    o_acc[hr:rows, :] = alpha * o_acc[hr:rows, :] + pv[hr:rows]

    @pl.when(j == nj - 1)
    def _():
        lam = _diff_lambda(lam_ref, lam_init)
        lane_half = lax.broadcasted_iota(jnp.int32, (nq, LANES), 1) // HEAD_DIM
        outs = []
        for u in range(nu):
            blk = o_acc[u * 2 * nq:(u + 1) * 2 * nq, u * LANES:(u + 1) * LANES]
            if u < nu // 2:
                outs.append(jnp.where(lane_half == 0, blk[0:nq], blk[nq:2 * nq]))
            else:
                r0 = u * 2 * nq - hr
                l = l_s[r0:r0 + 2 * nq, :]
                l = jnp.where(l > 0, l, 1.0)
                o = blk[0:nq] / l[0:nq] - lam * (blk[nq:2 * nq] / l[nq:2 * nq])
                outs.append(_rms(o, sg_ref[...]) * (1.0 - lam_init))
        o_ref[0] = jnp.concatenate(outs, axis=1)


def _ab_sample_attn(qz, kv_new, cache, page_table, lam, subln_g, lam_init, npg):
    db, nq, w = qz.shape
    n_pages = page_table.shape[1]
    page = cache.shape[1]
    kvw = kv_new.shape[2]
    nu = w // LANES
    rows = nu * 2 * nq
    past = n_pages * page
    nj = n_pages // npg
    cache_t = jnp.transpose(cache, (0, 2, 3, 4, 1)).reshape(cache.shape[0], 2, kvw // 2, page)
    kern = functools.partial(_ab_sample_kernel, npg=npg, page=page, nq=nq, past=past, lam_init=lam_init)

    def page_spec(p):
        return pl.BlockSpec((1, 2, kvw // 2, page),
                            lambda b, j, pt: (pt[b, n_pages - 1 - (j * npg + p)], 0, 0, 0))

    grid_spec = pltpu.PrefetchScalarGridSpec(
        num_scalar_prefetch=1,
        grid=(db, nj),
        in_specs=[
            pl.BlockSpec(lam.shape, lambda b, j, pt: (0, 0)),
            pl.BlockSpec((1, LANES), lambda b, j, pt: (0, 0)),
            pl.BlockSpec((2 * page, page), lambda b, j, pt: (0, 0)),
            pl.BlockSpec((1, nq, w), lambda b, j, pt: (b, 0, 0)),
            pl.BlockSpec((1, nq, kvw), lambda b, j, pt: (b, 0, 0)),
        ] + [page_spec(p) for p in range(npg)],
        out_specs=pl.BlockSpec((1, nq, w), lambda b, j, pt: (b, 0, 0)),
        scratch_shapes=[
            pltpu.VMEM((rows, w), F32), pltpu.VMEM((rows, w), BF16), pltpu.VMEM((page, kvw), F32),
            pltpu.VMEM((rows, w), F32), pltpu.VMEM((rows // 2, 1), F32),
            pltpu.VMEM((rows // 2, 1), F32), pltpu.VMEM((rows // 2, 1), F32),
        ],
    )
    return pl.pallas_call(
        kern,
        grid_spec=grid_spec,
        out_shape=jax.ShapeDtypeStruct((db, nq, w), F32),
        compiler_params=_cp("parallel", "arbitrary"),
        name="ab_sample_attn",
    )(page_table, lam, subln_g.reshape(1, LANES), _cumsum_matrix(page), qz, kv_new, *([cache_t] * npg))


def _mix_out_kernel(m_ref, x_ref, g_ref, wo_ref, wq_ref, x1_ref, qm_ref):
    y = _dot(m_ref[...].astype(BF16), wo_ref[...])
    x1 = x_ref[...] + _rms(y, g_ref[3:4, :])
    x1_ref[...] = x1
    qm_ref[...] = _dot(_rms(x1, g_ref[4:5, :]).astype(BF16), wq_ref[...])


def _mix_out(m, x, g, w_out, w_q):
    n, d = x.shape
    tm = _tile(n, 512)
    row = lambda i: (i, 0)
    full = lambda i: (0, 0)
    return pl.pallas_call(
        _mix_out_kernel,
        grid=(n // tm,),
        in_specs=[pl.BlockSpec((tm, m.shape[1]), row), pl.BlockSpec((tm, d), row), pl.BlockSpec(g.shape, full),
                  pl.BlockSpec(w_out.shape, full), pl.BlockSpec(w_q.shape, full)],
        out_specs=[pl.BlockSpec((tm, d), row), pl.BlockSpec((tm, w_q.shape[1]), row)],
        out_shape=[jax.ShapeDtypeStruct((n, d), F32), jax.ShapeDtypeStruct((n, w_q.shape[1]), F32)],
        compiler_params=_cp("parallel"),
        name="mix_out",
    )(m, x, g, w_out, w_q)


def _mem_attn_kernel(q_ref, kv_ref, o_ref, *, nh, hd, interleaved):
    q = q_ref[0]
    scale = 1.0 / math.sqrt(hd)
    outs = []
    for h in range(nh):
        qh = q[:, h * hd:(h + 1) * hd].astype(BF16)
        if interleaved:
            mlen = kv_ref.shape[1] // (2 * nh)
            kh = kv_ref[0, pl.ds(h, mlen, stride=2 * nh), :].astype(BF16)
            vh = kv_ref[0, pl.ds(nh + h, mlen, stride=2 * nh), :].astype(BF16)
        else:
            kh = kv_ref[0, :, h * hd:(h + 1) * hd].astype(BF16)
            vh = kv_ref[0, :, (nh + h) * hd:(nh + h + 1) * hd].astype(BF16)
        s = _dot_nt(qh, kh) * scale
        m = jnp.max(s, axis=1, keepdims=True)
        e = jnp.exp(s - m)
        p = e / jnp.sum(e, axis=1, keepdims=True)
        outs.append(_dot(p.astype(BF16), vh))
    o_ref[0] = jnp.concatenate(outs, axis=1)


def _mem_attn(q, kv, nh, interleaved=False):
    b, t, w = q.shape
    tq = _tile(t, 512)
    return pl.pallas_call(
        functools.partial(_mem_attn_kernel, nh=nh, hd=w // nh, interleaved=interleaved),
        grid=(b, t // tq),
        in_specs=[pl.BlockSpec((1, tq, w), lambda b_, i: (b_, i, 0)),
                  pl.BlockSpec((1,) + kv.shape[1:], lambda b_, i: (b_, 0, 0))],
        out_specs=pl.BlockSpec((1, tq, w), lambda b_, i: (b_, i, 0)),
        out_shape=jax.ShapeDtypeStruct((b, t, w), F32),
        compiler_params=_cp("parallel", "arbitrary"),
        name="mem_attn",
    )(q, kv)


def _mem_kv_kernel(x_ref, g_ref, w_ref, o_ref):
    o_ref[0] = _dot(_rms(x_ref[...], g_ref[0]).astype(BF16), w_ref[0])


def _mem_kv(mem, g, w_kv):
    n, d = mem.shape
    nl, _, wo = w_kv.shape
    tm = _tile(n, 512)
    return pl.pallas_call(
        _mem_kv_kernel,
        grid=(nl, n // tm),
        in_specs=[pl.BlockSpec((tm, d), lambda l, i: (i, 0)), pl.BlockSpec((1, 1, d), lambda l, i: (l, 0, 0)),
                  pl.BlockSpec((1, d, wo), lambda l, i: (l, 0, 0))],
        out_specs=pl.BlockSpec((1, tm, wo), lambda l, i: (l, i, 0)),
        out_shape=jax.ShapeDtypeStruct((nl, n, wo), F32),
        compiler_params=_cp("parallel", "parallel"),
        name="mem_kv",
    )(mem, g, w_kv)


def _nsa_proj_kernel(x_ref, g_ref, w_ref, c_ref, s1_ref, s2_ref, qq_ref, prow_ref, wrow_ref, gate_ref, *, qw):
    h = _rms(x_ref[...], g_ref[2:3, :]).astype(BF16)
    c, s1, s2 = c_ref[...], s1_ref[...], s2_ref[...]
    hw = qw // 2
    for i in range(2):
        z = _dot(h, w_ref[:, i * hw:(i + 1) * hw])
        qq_ref[:, i * hw:(i + 1) * hw] = z
        qq_ref[:, qw + i * hw:qw + (i + 1) * hw] = _rope_cols(z, c, s1, s2)
    zr = _dot(h, w_ref[:, qw:qw + 6 * LANES])
    prow_ref[:, 0:2 * LANES] = zr[:, 0:2 * LANES]
    prow_ref[:, 2 * LANES:3 * LANES] = _rope_cols(zr[:, 2 * LANES:3 * LANES], c, s1, s2)
    prow_ref[:, 3 * LANES:4 * LANES] = zr[:, 3 * LANES:4 * LANES]
    wrow_ref[:, 0:LANES] = _rope_cols(zr[:, 4 * LANES:5 * LANES], c, s1, s2)
    wrow_ref[:, LANES:2 * LANES] = zr[:, 5 * LANES:6 * LANES]
    gate_ref[...] = jax.nn.sigmoid(_dot(h, w_ref[:, qw + 6 * LANES:qw + 7 * LANES]))


def _nsa_proj(x, g, w_in, qw, tabs, tab_map, tm):
    n, d = x.shape
    row = lambda i: (i, 0)
    tspec = pl.BlockSpec((tm, LANES), tab_map)
    widths = (2 * qw, 4 * LANES, 2 * LANES, LANES)
    return pl.pallas_call(
        functools.partial(_nsa_proj_kernel, qw=qw),
        grid=(n // tm,),
        in_specs=[pl.BlockSpec((tm, d), row), pl.BlockSpec(g.shape, lambda i: (0, 0)),
                  pl.BlockSpec(w_in.shape, lambda i: (0, 0)), tspec, tspec, tspec],
        out_specs=[pl.BlockSpec((tm, w), row) for w in widths],
        out_shape=[jax.ShapeDtypeStruct((n, w), F32) for w in widths],
        compiler_params=_cp("parallel"),
        name="nsa_proj",
    )(x, g, w_in, *tabs)


def _compress_kernel(xk_ref, xv_ref, pe_ref, wk_ref, wv_ref, o_ref, *, mb):
    acck = jnp.zeros((mb, LANES), F32)
    accv = jnp.zeros((mb, LANES), F32)
    for l in range(BLK):
        ak = xk_ref[pl.ds(l, mb, stride=BLK), :] + pe_ref[l:l + 1, 0:LANES]
        av = xv_ref[pl.ds(l, mb, stride=BLK), :] + pe_ref[l:l + 1, LANES:2 * LANES]
        acck = acck + _dot(ak.astype(BF16), wk_ref[l])
        accv = accv + _dot(av.astype(BF16), wv_ref[l])
    o_ref[:, 0:LANES] = acck
    o_ref[:, LANES:2 * LANES] = accv


def _compress(rows2d, pe2, wk, wv):
    nblk = rows2d.shape[0] // BLK
    mb = _tile(nblk, 128)
    return pl.pallas_call(
        functools.partial(_compress_kernel, mb=mb),
        grid=(nblk // mb,),
        in_specs=[pl.BlockSpec((mb * BLK, LANES), lambda i: (i, 0)),
                  pl.BlockSpec((mb * BLK, LANES), lambda i: (i, 1)),
                  pl.BlockSpec(pe2.shape, lambda i: (0, 0)),
                  pl.BlockSpec(wk.shape, lambda i: (0, 0, 0)),
                  pl.BlockSpec(wv.shape, lambda i: (0, 0, 0))],
        out_specs=pl.BlockSpec((mb, 2 * LANES), lambda i: (i, 0)),
        out_shape=jax.ShapeDtypeStruct((nblk, 2 * LANES), F32),
        compiler_params=_cp("parallel"),
        name="nsa_compress",
    )(rows2d, rows2d, pe2, wk, wv)


def _compress_t_kernel(x_ref, pe_ref, w_ref, o_ref, *, pp):
    for c in range(2):
        acc = jnp.zeros((2 * pp, o_ref.shape[2]), F32)
        for d in range(HEAD_DIM):
            a = jnp.concatenate([x_ref[:, c * LANES + d, :], x_ref[:, c * LANES + HEAD_DIM + d, :]], axis=0)
            a = a + pe_ref[c * HEAD_DIM + d:c * HEAD_DIM + d + 1, :]
            acc = acc + _dot(a.astype(BF16), w_ref[c, d])
        o_ref[2 * c] = acc[0:pp]
        o_ref[2 * c + 1] = acc[pp:2 * pp]


def _compress_pool(cache_t, pe, phi):
    n_pool, _, _, page = cache_t.shape
    bpp = page // BLK
    x = cache_t.reshape(n_pool, 4 * LANES, page)
    pe_t = jnp.tile(jnp.transpose(pe, (0, 2, 1)), (1, 1, bpp)).reshape(2 * HEAD_DIM, page)
    ph = jnp.transpose(phi.reshape(2, BLK, HEAD_DIM, HEAD_DIM), (0, 2, 1, 3))
    eye = jnp.eye(bpp, dtype=F32)
    w = jnp.einsum('ab,cdle->cdalbe', eye, ph).reshape(2, HEAD_DIM, page, bpp * HEAD_DIM).astype(BF16)
    pp = _tile(n_pool, 128)
    out = pl.pallas_call(
        functools.partial(_compress_t_kernel, pp=pp),
        grid=(n_pool // pp,),
        in_specs=[pl.BlockSpec((pp, 2 * LANES, page), lambda i: (i, 0, 0)),
                  pl.BlockSpec(pe_t.shape, lambda i: (0, 0)),
                  pl.BlockSpec(w.shape, lambda i: (0, 0, 0, 0))],
        out_specs=pl.BlockSpec((4, pp, bpp * HEAD_DIM), lambda i: (0, i, 0)),
        out_shape=jax.ShapeDtypeStruct((4, n_pool, bpp * HEAD_DIM), F32),
        compiler_params=_cp("parallel"),
        name="nsa_compress_pool",
    )(x, pe_t, w)
    out = out.reshape(2, 2, n_pool, bpp, HEAD_DIM).transpose(2, 3, 0, 1, 4)
    return out.reshape(n_pool, bpp, 4 * HEAD_DIM)


def _compress_weights(pe, phi):
    pe2 = jnp.concatenate([pe[0], pe[0], pe[1], pe[1]], axis=1)
    ph = phi.reshape(2, BLK, HEAD_DIM, HEAD_DIM)
    z = jnp.zeros_like(ph[0])
    def bd(p):
        return jnp.concatenate([jnp.concatenate([p, z], axis=2), jnp.concatenate([z, p], axis=2)], axis=1).astype(BF16)
    return pe2, bd(ph[0]), bd(ph[1])


def _select_blocks(imp, nb):
    lane = lax.broadcasted_iota(jnp.int32, imp.shape, 1)
    cnt = jnp.zeros(imp.shape, F32)
    for i in range(nb):
        ci = imp[:, i:i + 1]
        ahead = (ci > imp) | ((ci == imp) & (lane > i))
        cnt = cnt + jnp.where(ahead, 1.0, 0.0)
    return jnp.where(cnt < float(min(TOPK, nb)), 1.0, 0.0)


def _masked_softmax(s, mask):
    s = jnp.where(mask, s, NEG)
    m = jnp.max(s, axis=-1, keepdims=True)
    e = jnp.where(mask, jnp.exp(s - m), 0.0)
    den = jnp.sum(e, axis=-1, keepdims=True)
    return e / jnp.where(den > 0, den, 1.0)


def _place(x, g):
    z = jnp.zeros_like(x)
    return jnp.concatenate([x, z] if g == 0 else [z, x], axis=1)


def _nsa_prompt_kernel(qq_ref, gate_ref, cmp_ref, e_ref, sv_ref, wv_ref, o_ref, bias_scr,
                       *, tq, kc, nb, nheads, wk):
    qi = pl.program_id(1)
    qw = nheads * HEAD_DIM
    hg = nheads // 2
    t = qi * tq + lax.broadcasted_iota(jnp.int32, (tq, 1), 0)
    qq = qq_ref[0] * QK_SCALE
    gates = gate_ref[0]
    cmp = cmp_ref[0]
    ck = cmp[:, 0:LANES].astype(BF16)
    cv = cmp[:, LANES:2 * LANES].astype(BF16)
    blk = lax.broadcasted_iota(jnp.int32, (tq, nb), 1)
    done = (blk + 1) * BLK - 1 <= t
    cur = t // BLK
    forced = (blk == cur) | (blk == 0)
    future = blk > cur
    nchunks = (qi * tq + tq - 1) // kc + 1
    wstart = pl.multiple_of(jnp.maximum(qi * tq - WINDOW, 0), tq)
    kwin = wv_ref[0, pl.ds(wstart, wk), 0:LANES].astype(BF16)
    vwin = wv_ref[0, pl.ds(wstart, wk), LANES:2 * LANES].astype(BF16)
    wpos = wstart + lax.broadcasted_iota(jnp.int32, (tq, wk), 1)
    wmask = (wpos <= t) & (wpos > t - WINDOW)
    kpos_c = lax.broadcasted_iota(jnp.int32, (tq, kc), 1)

    rows = hg * tq
    outs = []
    for g in range(2):
        def stack(off):
            return jnp.concatenate(
                [_place(qq[:, off + (g * hg + h8) * HEAD_DIM:off + (g * hg + h8 + 1) * HEAD_DIM], g)
                 for h8 in range(hg)], axis=0).astype(BF16)

        q = stack(0)
        qr = stack(qw)
        gcols = jnp.concatenate([gates[:, 3 * (g * hg + h8):3 * (g * hg + h8) + 3] for h8 in range(hg)], axis=0)

        s = _dot_nt(q, ck).reshape(hg, tq, nb)
        p = _masked_softmax(s, done[None])
        imp = jnp.sum(p, axis=0)
        o_cmp = _dot(p.reshape(rows, nb).astype(BF16), cv)
        imp = jnp.where(forced, FORCE, jnp.where(future, -1.0, imp))
        sel = _select_blocks(imp, nb).astype(BF16)

        def fill(c, _):
            allowed = (_dot(sel, e_ref[c]) > 0.5) & (c * kc + kpos_c <= t)
            bias_scr[c] = jnp.where(allowed, 0.0, NEG)
            return 0
        lax.fori_loop(0, nchunks, fill, 0)

        def body(c, carry):
            m, l, acc = carry
            start = pl.multiple_of(c * kc, kc)
            ks = sv_ref[0, pl.ds(start, kc), 0:LANES].astype(BF16)
            vs = sv_ref[0, pl.ds(start, kc), LANES:2 * LANES].astype(BF16)
            s = _dot_nt(qr, ks).reshape(hg, tq, kc) + bias_scr[c][None]
            s = s.reshape(rows, kc)
            m_new = jnp.maximum(m, jnp.max(s, axis=1, keepdims=True))
            alpha = jnp.exp(m - m_new)
            e = jnp.exp(s - m_new)
            l = alpha * l + jnp.sum(e, axis=1, keepdims=True)
            acc = alpha * acc + _dot(e.astype(BF16), vs)
            return m_new, l, acc

        _, l, acc = lax.fori_loop(
            0, nchunks, body,
            (jnp.full((rows, 1), NEG, F32), jnp.zeros((rows, 1), F32), jnp.zeros((rows, LANES), F32)))
        o_slc = acc / l

        sw = _dot_nt(qr, kwin).reshape(hg, tq, wk)
        pw = _masked_softmax(sw, wmask[None]).reshape(rows, wk)
        o_win = _dot(pw.astype(BF16), vwin)

        o = gcols[:, 0:1] * o_cmp + gcols[:, 1:2] * o_slc + gcols[:, 2:3] * o_win
        for h8 in range(hg):
            outs.append(o[h8 * tq:(h8 + 1) * tq, g * HEAD_DIM:(g + 1) * HEAD_DIM])
    o_ref[0] = jnp.concatenate(outs, axis=1)


def _block_expand(nb_rows, nkeys, kc):
    key = jnp.arange(nkeys).reshape(nkeys // kc, 1, kc)
    return (key // BLK == jnp.arange(nb_rows)[None, :, None]).astype(BF16)


def _nsa_prompt_attn(qq, gates, cmp, prow, wrow, nheads):
    b, t, _ = qq.shape
    qw = nheads * HEAD_DIM
    tq = _tile(t, 128)
    kc = _tile(t, 512)
    nb = t // BLK
    wk = min(t, tq + WINDOW)
    e = _block_expand(nb, t, kc)
    kern = functools.partial(_nsa_prompt_kernel, tq=tq, kc=kc, nb=nb, nheads=nheads, wk=wk)
    return pl.pallas_call(
        kern,
        grid=(b, t // tq),
        in_specs=[
            pl.BlockSpec((1, tq, 2 * qw), lambda b_, i: (b_, i, 0)),
            pl.BlockSpec((1, tq, LANES), lambda b_, i: (b_, i, 0)),
            pl.BlockSpec((1, nb, 2 * LANES), lambda b_, i: (b_, 0, 0)),
            pl.BlockSpec(e.shape, lambda b_, i: (0, 0, 0)),
            pl.BlockSpec((1, t, 2 * LANES), lambda b_, i: (b_, 0, 1)),
            pl.BlockSpec((1, t, 2 * LANES), lambda b_, i: (b_, 0, 0)),
        ],
        out_specs=pl.BlockSpec((1, tq, qw), lambda b_, i: (b_, i, 0)),
        out_shape=jax.ShapeDtypeStruct((b, t, qw), F32),
        scratch_shapes=[pltpu.VMEM((t // kc, tq, kc), F32)],
        compiler_params=_cp("parallel", "arbitrary"),
        name="nsa_prompt_attn",
    )(qq, gates, cmp, e, prow, wrow)


def _nsa_sample_kernel(pt_ref, q_ref, qr_ref, gate_ref, cnew_ref, e_ref, new_ref, win_ref, wnew_ref,
                       *rest, n_pages, page, nq, hg, nbp, past):
    pages = rest[:n_pages]
    cmps = rest[n_pages:2 * n_pages]
    o_ref = rest[2 * n_pages]
    cmp_scr = rest[2 * n_pages + 1]
    rows = 2 * nq * hg
    bpp = page // BLK
    nb = n_pages * bpp + 1
    nnew = new_ref.shape[1]
    nwin = win_ref.shape[3]

    cmp_scr[...] = jnp.zeros_like(cmp_scr)
    for p in range(n_pages):
        cmp_scr[p * bpp:(p + 1) * bpp, :] = cmps[p][0]
    cmp_scr[n_pages * bpp:n_pages * bpp + 1, :] = cnew_ref[0, 0:1, :]
    cmp = cmp_scr[...]
    ck = cmp[:, 0:LANES].astype(BF16)
    cv = cmp[:, LANES:2 * LANES].astype(BF16)

    q = q_ref[0].astype(BF16)
    qr = qr_ref[0].astype(BF16)
    gates = gate_ref[0]
    rowi = lax.broadcasted_iota(jnp.int32, (rows, 1), 0)
    t = past + (rowi // hg) % nq
    top = rowi < rows // 2

    def pick(x):
        return jnp.where(top, x[:, 0:HEAD_DIM], x[:, HEAD_DIM:2 * HEAD_DIM])

    blk = lax.broadcasted_iota(jnp.int32, (rows, nbp), 1)
    done = ((blk + 1) * BLK - 1 <= t) & (blk < nb)
    p_cmp = _masked_softmax(_dot_nt(q, ck) * QK_SCALE, done)
    o_cmp = pick(_dot(p_cmp.astype(BF16), cv))

    ng = rows // hg
    imp = jnp.sum(p_cmp.reshape(ng, hg, nbp), axis=1)
    tg = past + lax.broadcasted_iota(jnp.int32, (ng, 1), 0) % nq
    blk_g = lax.broadcasted_iota(jnp.int32, (ng, nbp), 1)
    cur = tg // BLK
    imp = jnp.where((blk_g == cur) | (blk_g == 0), FORCE, jnp.where(blk_g > cur, -1.0, imp))
    sel = _select_blocks(imp, nb)
    expand = (lax.broadcasted_iota(jnp.int32, (rows, ng), 0) // hg
              == lax.broadcasted_iota(jnp.int32, (rows, ng), 1)).astype(F32)
    sel_f = _dot(expand, sel)
    sel_rows = sel_f.astype(BF16)

    allowed = _dot(sel_rows, e_ref[...]) > 0.5
    s_parts = [_dot(qr, pages[p][0, 0].astype(BF16)) for p in range(n_pages)]
    s = jnp.concatenate(s_parts, axis=1) * QK_SCALE
    s = jnp.where(allowed, s, NEG)
    new = new_ref[0]
    s_n = _dot_nt(qr, new[:, 0:LANES].astype(BF16)) * QK_SCALE
    npos = past + lax.broadcasted_iota(jnp.int32, (rows, nnew), 1)
    mask_n = (npos <= t) & (sel_f[:, nb - 1:nb] > 0.5)
    s_n = jnp.where(mask_n, s_n, NEG)
    m = jnp.maximum(jnp.max(s, axis=1, keepdims=True), jnp.max(s_n, axis=1, keepdims=True))
    e = jnp.where(allowed, jnp.exp(s - m), 0.0)
    e_n = jnp.where(mask_n, jnp.exp(s_n - m), 0.0)
    den = jnp.sum(e, axis=1, keepdims=True) + jnp.sum(e_n, axis=1, keepdims=True)
    acc = _dot(e_n.astype(BF16), new[:, LANES:2 * LANES].astype(BF16))
    for p in range(n_pages):
        acc = acc + _dot_nt(e[:, p * page:(p + 1) * page].astype(BF16), pages[p][0, 1].astype(BF16))
    o_slc = pick(acc / jnp.where(den > 0, den, 1.0))

    wnew = wnew_ref[0]
    s_w = _dot(qr, win_ref[0, 0].astype(BF16)) * QK_SCALE
    wpos = past - nwin + lax.broadcasted_iota(jnp.int32, (rows, nwin), 1)
    mask_w = (wpos <= t) & (wpos > t - WINDOW) & (wpos >= 0)
    s_w = jnp.where(mask_w, s_w, NEG)
    s_wn = _dot_nt(qr, wnew[:, 0:LANES].astype(BF16)) * QK_SCALE
    mask_wn = (npos <= t) & (npos > t - WINDOW)
    s_wn = jnp.where(mask_wn, s_wn, NEG)
    m = jnp.maximum(jnp.max(s_w, axis=1, keepdims=True), jnp.max(s_wn, axis=1, keepdims=True))
    e = jnp.where(mask_w, jnp.exp(s_w - m), 0.0)
    e_n = jnp.where(mask_wn, jnp.exp(s_wn - m), 0.0)
    den = jnp.sum(e, axis=1, keepdims=True) + jnp.sum(e_n, axis=1, keepdims=True)
    acc = _dot_nt(e.astype(BF16), win_ref[0, 1].astype(BF16)) \
        + _dot(e_n.astype(BF16), wnew[:, LANES:2 * LANES].astype(BF16))
    o_win = pick(acc / jnp.where(den > 0, den, 1.0))

    o_ref[0] = gates[:, 0:1] * o_cmp + gates[:, 1:2] * o_slc + gates[:, 2:3] * o_win


def _nsa_sample_attn(q, q_rot, gates, cmp_pool, cmp_new, cache, new_blk, win_state, wnew, page_table, nq):
    db, rows, _ = q.shape
    n_pages = page_table.shape[1]
    page = cache.shape[3]
    past = n_pages * page
    nb = n_pages * (page // BLK) + 1
    nbp = -(-nb // 8) * 8
    nkeys = n_pages * page
    e = (jnp.arange(nkeys)[None, :] // BLK == jnp.arange(nbp)[:, None]).astype(BF16)
    nnew = new_blk.shape[1]
    kern = functools.partial(_nsa_sample_kernel, n_pages=n_pages, page=page, nq=nq, hg=rows // (2 * nq),
                             nbp=nbp, past=past)

    bpp = page // BLK
    cmp_pool3 = cmp_pool

    def page_spec(p):
        return pl.BlockSpec((1, 2, LANES, page), lambda b, pt: (pt[b, p], 1, 0, 0))

    def cmp_spec(p):
        return pl.BlockSpec((1, bpp, 2 * LANES), lambda b, pt: (pt[b, p], 0, 0))

    grid_spec = pltpu.PrefetchScalarGridSpec(
        num_scalar_prefetch=1,
        grid=(db,),
        in_specs=[
            pl.BlockSpec((1, rows, LANES), lambda b, pt: (b, 0, 0)),
            pl.BlockSpec((1, rows, LANES), lambda b, pt: (b, 0, 0)),
            pl.BlockSpec((1, rows, 3), lambda b, pt: (b, 0, 0)),
            pl.BlockSpec((1, 1, 2 * LANES), lambda b, pt: (b, 0, 0)),
            pl.BlockSpec(e.shape, lambda b, pt: (0, 0)),
            pl.BlockSpec((1, nnew, 2 * LANES), lambda b, pt: (b, 0, 1)),
            pl.BlockSpec((1,) + win_state.shape[1:], lambda b, pt: (b, 0, 0, 0)),
            pl.BlockSpec((1, nnew, 2 * LANES), lambda b, pt: (b, 0, 0)),
        ] + [page_spec(p) for p in range(n_pages)] + [cmp_spec(p) for p in range(n_pages)],
        out_specs=pl.BlockSpec((1, rows, HEAD_DIM), lambda b, pt: (b, 0, 0)),
        scratch_shapes=[pltpu.VMEM((nbp, 2 * LANES), F32)],
    )
    return pl.pallas_call(
        kern,
        grid_spec=grid_spec,
        out_shape=jax.ShapeDtypeStruct((db, rows, HEAD_DIM), F32),
        compiler_params=_cp("arbitrary"),
        name="nsa_sample_attn",
    )(page_table, q, q_rot, gates, cmp_new, e, new_blk, win_state, wnew,
      *([cache] * n_pages), *([cmp_pool3] * n_pages))


def kernel(x_prompt, x_sample, mem_prompt, page_table, cache_kv_l0, cache_mem_l0, cache_nsa_l1, state_win_l1,
           cache_mem_l1, cache_kv_l2, cache_mem_l2, cache_nsa_l3, state_win_l3, cache_mem_l3, norm_g, mem_norm_g,
           ffn_w_in, ffn_w_out, w_q_mem, w_kv_mem, w_o_mem, ab_w_in, ab_w_out, diff_lambda, diff_subln_g,
           nsa_w_in, nsa_w_out, nsa_cmp_pe, nsa_cmp_phi):
    bsz, seq, d = x_prompt.shape
    db, dq, _ = x_sample.shape
    depth = norm_g.shape[0]
    n_pages = page_table.shape[1]
    page = cache_kv_l0.shape[1]
    past = n_pages * page
    mlen = mem_prompt.shape[1]
    mem_w = w_q_mem.shape[2]
    mem_heads = mem_w // LANES
    nheads = d // HEAD_DIM
    qw = nheads * HEAD_DIM
    kv_caches = (cache_kv_l0, cache_kv_l2)
    nsa_caches = (cache_nsa_l1, cache_nsa_l3)
    win_states = (state_win_l1, state_win_l3)
    mem_caches = (cache_mem_l0, cache_mem_l1, cache_mem_l2, cache_mem_l3)

    np_, ns_ = bsz * seq, db * dq
    tm_p = _tile(seq, 512)
    tm_s = ns_
    tabs_p = _rope_tables(jnp.arange(seq, dtype=jnp.int32))
    tabs_s = _rope_tables(past + jnp.arange(ns_, dtype=jnp.int32) % dq)
    spt = seq // tm_p
    map_p = lambda i: (i % spt, 0)
    map_s = lambda i: (0, 0)

    bf = lambda w: w.astype(BF16)
    ffn_in, ffn_out = bf(ffn_w_in), bf(ffn_w_out)
    wq_m, wkv_m, wo_m = bf(w_q_mem), bf(w_kv_mem), bf(w_o_mem)
    ab_in, ab_out = bf(ab_w_in), bf(ab_w_out)
    nsa_in = bf(jnp.pad(nsa_w_in, ((0, 0), (0, 0), (0, qw + 7 * LANES - nsa_w_in.shape[2]))))
    nsa_out = bf(nsa_w_out)

    mkv_all = _mem_kv(mem_prompt.reshape(bsz * mlen, d), mem_norm_g.reshape(depth, 1, d), wkv_m)

    xp = x_prompt.reshape(np_, d)
    xs = x_sample.reshape(ns_, d)
    kv_p, kv_s, nsa_p, nsa_s, win_p, win_s, mem_p = [], [], [], [], [], [], []
    om_p = om_s = None
    for layer in range(depth):
        g = norm_g[layer]
        i = layer // 2
        if layer == 0:
            xp = _ffn_half(xp, g, ffn_in[layer, 0], ffn_out[layer, 0], 0, 1)
            xs = _ffn_half(xs, g, ffn_in[layer, 0], ffn_out[layer, 0], 0, 1)
        if layer % 2 == 0:
            lam_init = 0.8 - 0.6 * math.exp(-0.3 * layer)
            lam = diff_lambda[i]
            sg = diff_subln_g[i]
            qz_p, kvr_p = _ab_proj(xp, g, ab_in[i], tabs_p, map_p, tm_p)
            qz_s, kvr_s = _ab_proj(xs, g, ab_in[i], tabs_s, map_s, tm_s)
            kvw = kvr_p.shape[1]
            mp = _ab_prompt_attn(qz_p.reshape(bsz, seq, -1), kvr_p.reshape(bsz, seq, kvw), lam, sg, lam_init)
            ms = _ab_sample_attn(qz_s.reshape(db, dq, -1), kvr_s.reshape(db, dq, kvw), kv_caches[i], page_table,
                                 lam, sg, lam_init, npg=_tile(n_pages, 8))
            kv_p.append(kvr_p.reshape(bsz, seq, 2, kvw // (2 * HEAD_DIM), HEAD_DIM))
            kv_s.append(kvr_s.reshape(db, dq, 2, kvw // (2 * HEAD_DIM), HEAD_DIM))
            mp = mp.reshape(np_, -1)
            ms = ms.reshape(ns_, -1)
            w_mix = ab_out[i]
        else:
            qq_p, prow_p, wrow_p, gate_p = _nsa_proj(xp, g, nsa_in[i], qw, tabs_p, map_p, tm_p)
            qq_s, prow_s, wrow_s, gate_s = _nsa_proj(xs, g, nsa_in[i], qw, tabs_s, map_s, tm_s)
            pe2, wk, wv = _compress_weights(nsa_cmp_pe[i], nsa_cmp_phi[i])
            cmp_p = _compress(prow_p, pe2, wk, wv).reshape(bsz, seq // BLK, 2 * LANES)
            mp = _nsa_prompt_attn(qq_p.reshape(bsz, seq, -1), gate_p.reshape(bsz, seq, LANES), cmp_p,
                                  prow_p.reshape(bsz, seq, -1), wrow_p.reshape(bsz, seq, -1), nheads)
            mp = mp.reshape(np_, -1)
            cache = nsa_caches[i]
            n_pool = cache.shape[0]
            roww = prow_s.shape[1]
            cache_t = jnp.transpose(cache, (0, 2, 3, 4, 1)).reshape(n_pool, 4, LANES, page)
            cmp_pool = _compress_pool(cache_t, nsa_cmp_pe[i], nsa_cmp_phi[i])
            new_blk = jnp.pad(prow_s.reshape(db, dq, roww), ((0, 0), (0, BLK - dq), (0, 0)))
            cmp_new = _compress(new_blk.reshape(db * BLK, roww), pe2, wk, wv).reshape(db, 1, 2 * LANES)
            wnew = jnp.pad(wrow_s.reshape(db, dq, -1), ((0, 0), (0, BLK - dq), (0, 0)))
            nwin = win_states[i].shape[1]
            win_state = jnp.transpose(win_states[i], (0, 2, 3, 4, 1)).reshape(db, 2, LANES, nwin)
            hg = nheads // 2

            def to_rows(a, pad):
                w = a.shape[1] // nheads
                a = a.reshape(db, dq, 2, hg, w).transpose(0, 2, 1, 3, 4)
                if pad:
                    z = jnp.zeros_like(a[:, 0:1])
                    a = jnp.concatenate([jnp.concatenate([a[:, 0:1], z], axis=-1),
                                         jnp.concatenate([z, a[:, 1:2]], axis=-1)], axis=1)
                return a.reshape(db, 2 * dq * hg, -1)

            q_rows = to_rows(qq_s[:, 0:qw], True)
            qr_rows = to_rows(qq_s[:, qw:2 * qw], True)
            g_rows = to_rows(gate_s[:, 0:3 * nheads], False)
            o_rows = _nsa_sample_attn(q_rows, qr_rows, g_rows, cmp_pool, cmp_new, cache_t, new_blk, win_state, wnew,
                                      page_table, dq)
            ms = o_rows.reshape(db, 2, dq, hg, HEAD_DIM).transpose(0, 2, 1, 3, 4).reshape(ns_, qw)
            nsa_p.append(prow_p.reshape(bsz, seq, 4, 2, HEAD_DIM))
            nsa_s.append(prow_s.reshape(db, dq, 4, 2, HEAD_DIM))
            n_keep = min(WINDOW, seq)
            win_p.append(wrow_p.reshape(bsz, seq, 2, 2, HEAD_DIM)[:, seq - n_keep:])
            win_all = jnp.concatenate([win_states[i], wrow_s.reshape(db, dq, 2, 2, HEAD_DIM)], axis=1)
            win_s.append(win_all[:, dq:])
            w_mix = nsa_out[i]

        xp, qm_p = _mix_out(mp, xp, g, w_mix, wq_m[layer])
        xs, qm_s = _mix_out(ms, xs, g, w_mix, wq_m[layer])
        mkv = mkv_all[layer].reshape(bsz, mlen, 2 * mem_w)
        om_p = _mem_attn(qm_p.reshape(bsz, seq, mem_w), mkv, mem_heads).reshape(np_, mem_w)
        om_s = _mem_attn(qm_s.reshape(db, dq, mem_w), mem_caches[layer].reshape(db, mlen * 2 * mem_heads, LANES),
                         mem_heads, interleaved=True).reshape(ns_, mem_w)
        mem_p.append(mkv.reshape(bsz, mlen, 2, mem_heads, LANES))
        xp = _ffn_half(xp, g, ffn_in[layer, 1], ffn_out[layer, 1], 6, 7, om_p, wo_m[layer], 5)
        xs = _ffn_half(xs, g, ffn_in[layer, 1], ffn_out[layer, 1], 6, 7, om_s, wo_m[layer], 5)
        if layer + 1 < depth:
            gn = norm_g[layer + 1]
            xp = _ffn_half(xp, gn, ffn_in[layer + 1, 0], ffn_out[layer + 1, 0], 0, 1)
            xs = _ffn_half(xs, gn, ffn_in[layer + 1, 0], ffn_out[layer + 1, 0], 0, 1)

    return (xp.reshape(bsz, seq, d), xs.reshape(db, dq, d),
            kv_p[0], kv_s[0], mem_p[0],
            nsa_p[0], nsa_s[0], win_p[0], win_s[0], mem_p[1],
            kv_p[1], kv_s[1], mem_p[2],
            nsa_p[1], nsa_s[1], win_p[1], win_s[1], mem_p[3])
```

```python
import functools
import math

import jax
import jax.numpy as jnp
from jax import lax
from jax.experimental import pallas as pl
from jax.experimental.pallas import tpu as pltpu

F32 = jnp.float32
BF16 = jnp.bfloat16

HEAD_DIM = 64
ROT_DIM = HEAD_DIM // 4
ROPE_THETA = 500000.0
BLK = 64
TOPK = 8
WINDOW = 256
EPS = 1e-6
NEG = -1e30
FORCE = 1e4
LANES = 128
QK_SCALE = 1.0 / math.sqrt(HEAD_DIM)

VMEM_LIMIT = 56 * 1024 * 1024


def _cp(*sem):
    return pltpu.CompilerParams(dimension_semantics=sem, vmem_limit_bytes=VMEM_LIMIT)


def _rms(x, g):
    return x * lax.rsqrt(jnp.mean(x * x, axis=-1, keepdims=True) + EPS) * g


def _dot(a, b):
    return jnp.dot(a, b, preferred_element_type=F32)


def _dot_nt(a, b):
    return lax.dot_general(a, b, (((1,), (1,)), ((), ())), preferred_element_type=F32)


def _softplus(z):
    return jnp.maximum(z, 0.0) + jnp.log(1.0 + jnp.exp(-jnp.abs(z)))


def _split_bf16(x):
    hi = x.astype(BF16)
    lo = (x - hi.astype(F32)).astype(BF16)
    return hi, lo


def _tile(n, pref):
    t = min(n, pref)
    while n % t:
        t //= 2
    return t


def _rope_tables(pos):
    half = ROT_DIM // 2
    inv = jnp.power(ROPE_THETA, -jnp.arange(half, dtype=F32) * 2.0 / ROT_DIM)
    ang = pos.astype(F32)[:, None] * inv[None, :]
    cos = jnp.cos(ang)
    sin = jnp.sin(ang)
    p = pos.shape[0]
    ones = jnp.ones((p, HEAD_DIM - ROT_DIM), F32)
    zeros = jnp.zeros((p, HEAD_DIM - ROT_DIM), F32)
    zh = jnp.zeros((p, half), F32)
    c = jnp.concatenate([cos, cos, ones], axis=1)
    s1 = jnp.concatenate([-sin, zh, zeros], axis=1)
    s2 = jnp.concatenate([zh, sin, zeros], axis=1)
    rep = LANES // HEAD_DIM
    return jnp.tile(c, (1, rep)), jnp.tile(s1, (1, rep)), jnp.tile(s2, (1, rep))


def _rope_cols(z, c, s1, s2):
    outs = []
    for i in range(z.shape[1] // LANES):
        xb = z[:, i * LANES:(i + 1) * LANES]
        outs.append(xb * c + pltpu.roll(xb, LANES - ROT_DIM // 2, 1) * s1 + pltpu.roll(xb, ROT_DIM // 2, 1) * s2)
    return outs[0] if len(outs) == 1 else jnp.concatenate(outs, axis=1)


def _ffn_kernel(*refs, nj, pre, ipre, ipost, imem):
    if pre:
        x_ref, om_ref, wom_ref, g_ref, wg_ref, wu_ref, wo_ref, o_ref, h_scr, acc_scr, x_scr = refs
    else:
        x_ref, g_ref, wg_ref, wu_ref, wo_ref, o_ref, h_scr, acc_scr, x_scr = refs
    j = pl.program_id(1)

    @pl.when(j == 0)
    def _():
        x = x_ref[...]
        if pre:
            y = _dot(om_ref[...].astype(BF16), wom_ref[...])
            x = x + _rms(y, g_ref[imem:imem + 1, :])
        x_scr[...] = x
        h_scr[...] = _rms(x, g_ref[ipre:ipre + 1, :]).astype(BF16)
        acc_scr[...] = jnp.zeros_like(acc_scr)

    h = h_scr[...]
    gate = _dot(h, wg_ref[0])
    up = _dot(h, wu_ref[0])
    act = (gate * jax.nn.sigmoid(gate)) * up
    acc_scr[...] += _dot(act.astype(BF16), wo_ref[...])

    @pl.when(j == nj - 1)
    def _():
        o_ref[...] = x_scr[...] + 0.5 * _rms(acc_scr[...], g_ref[ipost:ipost + 1, :])


def _ffn_chunks(w_in, dff):
    tf = dff // 2 if (dff // 2) % LANES == 0 else dff
    lead, d = w_in.shape[:-2], w_in.shape[-2]
    w = w_in.reshape(lead + (d, 2 * dff // tf, tf))
    return jnp.moveaxis(w, -2, -3)


def _ffn_half(x, g, w_in, w_out, ipre, ipost, om=None, w_om=None, imem=None):
    n, d = x.shape
    dff = w_out.shape[0]
    tm = _tile(n, 512)
    tf = w_in.shape[2]
    nj = dff // tf
    pre = om is not None
    kern = functools.partial(_ffn_kernel, nj=nj, pre=pre, ipre=ipre, ipost=ipost, imem=imem)
    row = lambda i, j: (i, 0)
    in_specs = [pl.BlockSpec((tm, d), row)]
    args = [x]
    if pre:
        in_specs += [pl.BlockSpec((tm, om.shape[1]), row), pl.BlockSpec(w_om.shape, lambda i, j: (0, 0))]
        args += [om, w_om]
    in_specs += [
        pl.BlockSpec(g.shape, lambda i, j: (0, 0)),
        pl.BlockSpec((1, d, tf), lambda i, j: (j, 0, 0)),
        pl.BlockSpec((1, d, tf), lambda i, j: (nj + j, 0, 0)),
        pl.BlockSpec((tf, d), lambda i, j: (j, 0)),
    ]
    args += [g, w_in, w_in, w_out]
    return pl.pallas_call(
        kern,
        grid=(n // tm, nj),
        in_specs=in_specs,
        out_specs=pl.BlockSpec((tm, d), row),
        out_shape=jax.ShapeDtypeStruct((n, d), F32),
        scratch_shapes=[pltpu.VMEM((tm, d), BF16), pltpu.VMEM((tm, d), F32), pltpu.VMEM((tm, d), F32)],
        compiler_params=_cp("parallel", "arbitrary"),
        name="ffn_half",
    )(*args)


def _ab_proj_kernel(x_ref, g_ref, w_ref, c_ref, s1_ref, s2_ref, qz_ref, kv_ref, *, sbw):
    h = _rms(x_ref[...], g_ref[2:3, :]).astype(BF16)
    c, s1, s2 = c_ref[...], s1_ref[...], s2_ref[...]

    def proj(i):
        return _dot(h, w_ref[:, i * sbw:(i + 1) * sbw])

    qz_ref[:, 0:sbw] = proj(0)
    kv_ref[:, 0:sbw] = proj(1)
    kv_ref[:, 2 * sbw:3 * sbw] = proj(2)
    qz_ref[:, sbw:2 * sbw] = _rope_cols(proj(3), c, s1, s2)
    kv_ref[:, sbw:2 * sbw] = _rope_cols(proj(4), c, s1, s2)
    kv_ref[:, 3 * sbw:4 * sbw] = proj(5)


def _ab_proj(x, g, w_in, tabs, tab_map, tm):
    n, d = x.shape
    sbw = w_in.shape[1] // 6
    row = lambda i: (i, 0)
    tspec = pl.BlockSpec((tm, LANES), tab_map)
    return pl.pallas_call(
        functools.partial(_ab_proj_kernel, sbw=sbw),
        grid=(n // tm,),
        in_specs=[pl.BlockSpec((tm, d), row), pl.BlockSpec(g.shape, lambda i: (0, 0)),
                  pl.BlockSpec(w_in.shape, lambda i: (0, 0)), tspec, tspec, tspec],
        out_specs=[pl.BlockSpec((tm, 2 * sbw), row), pl.BlockSpec((tm, 4 * sbw), row)],
        out_shape=[jax.ShapeDtypeStruct((n, 2 * sbw), F32), jax.ShapeDtypeStruct((n, 4 * sbw), F32)],
        compiler_params=_cp("parallel"),
        name="ab_proj",
    )(x, g, w_in, *tabs)


def _diff_lambda(lam_ref, lam_init):
    lv = lam_ref[...]
    a = jnp.sum(lv[0:1, :] * lv[1:2, :], axis=1, keepdims=True)
    b = jnp.sum(lv[2:3, :] * lv[3:4, :], axis=1, keepdims=True)
    return jnp.exp(a) - jnp.exp(b) + lam_init


def _ab_prompt_kernel(lam_ref, sg_ref, uu_ref, q_ref, k_ref, v_ref, o_ref, *, tq, rs, n_sb, lam_init):
    u = pl.program_id(1)
    qi = pl.program_id(2)
    q = q_ref[0]
    lane_half = lax.broadcasted_iota(jnp.int32, (tq, LANES), 1) // HEAD_DIM
    qsc = q * QK_SCALE
    qs = jnp.concatenate([jnp.where(lane_half == 0, qsc, 0.0), jnp.where(lane_half == 1, qsc, 0.0)],
                         axis=0).astype(BF16)
    nst = 2 * tq // rs
    strips = [qs[i * rs:(i + 1) * rs] for i in range(nst)]
    row = lax.broadcasted_iota(jnp.int32, (rs, tq), 0)
    col = lax.broadcasted_iota(jnp.int32, (rs, tq), 1)

    def load_kv(kb):
        start = pl.multiple_of(kb * tq, tq)
        return k_ref[0, pl.ds(start, tq), :].astype(BF16), v_ref[0, pl.ds(start, tq), :].astype(BF16)

    @pl.when(u < n_sb)
    def _():
        uu = uu_ref[...]

        def step(kb, carry, diagonal):
            runs, accs = carry
            k, v = load_kv(kb)
            new_runs, new_accs = [], []
            for i in range(nst):
                z = _dot_nt(strips[i], k)
                sp = _softplus(z)
                if diagonal:
                    mask = col < (row + i * rs) % tq
                    lk = jnp.where(mask, -sp, 0.0)
                else:
                    lk = -sp
                hi, lo = _split_bf16(lk)
                later = _dot(jnp.concatenate([hi, lo], axis=1), uu) + runs[i]
                w = jnp.exp(z - sp + later)
                if diagonal:
                    w = jnp.where(mask, w, 0.0)
                new_accs.append(accs[i] + _dot(w.astype(BF16), v))
                new_runs.append(runs[i] + jnp.sum(lk, axis=1, keepdims=True))
            return tuple(new_runs), tuple(new_accs)

        init = (tuple(jnp.zeros((rs, 1), F32) for _ in range(nst)),
                tuple(jnp.zeros((rs, LANES), F32) for _ in range(nst)))
        carry = step(qi, init, True)
        _, accs = lax.fori_loop(1, qi + 1, lambda i, c: step(qi - i, c, False), carry)
        acc = jnp.concatenate(accs, axis=0)
        o_ref[0] = jnp.where(lane_half == 0, acc[0:tq], acc[tq:2 * tq])

    @pl.when(u >= n_sb)
    def _():
        lam = _diff_lambda(lam_ref, lam_init)

        def step(kb, carry, diagonal):
            ms, ls, accs = carry
            k, v = load_kv(kb)
            nm, nl, na = [], [], []
            for i in range(nst):
                s = _dot_nt(strips[i], k)
                if diagonal:
                    mask = col <= (row + i * rs) % tq
                    s = jnp.where(mask, s, NEG)
                m_new = jnp.maximum(ms[i], jnp.max(s, axis=1, keepdims=True))
                alpha = jnp.exp(ms[i] - m_new)
                e = jnp.exp(s - m_new)
                if diagonal:
                    e = jnp.where(mask, e, 0.0)
                nm.append(m_new)
                nl.append(alpha * ls[i] + jnp.sum(e, axis=1, keepdims=True))
                na.append(alpha * accs[i] + _dot(e.astype(BF16), v))
            return tuple(nm), tuple(nl), tuple(na)

        init = (tuple(jnp.full((rs, 1), NEG, F32) for _ in range(nst)),
                tuple(jnp.zeros((rs, 1), F32) for _ in range(nst)),
                tuple(jnp.zeros((rs, LANES), F32) for _ in range(nst)))
        carry = step(qi, init, True)
        _, ls, accs = lax.fori_loop(0, qi, lambda kb, c: step(kb, c, False), carry)
        p = jnp.concatenate(accs, axis=0) / jnp.concatenate(ls, axis=0)
        o = p[0:tq] - lam * p[tq:2 * tq]
        o_ref[0] = _rms(o, sg_ref[...]) * (1.0 - lam_init)


def _cumsum_matrix(tk):
    j = jnp.arange(2 * tk)[:, None] % tk
    s = jnp.arange(tk)[None, :]
    return (j > s).astype(BF16)


def _ab_prompt_attn(qz, kv, lam, subln_g, lam_init, rs=None):
    b, t, w = qz.shape
    nu = w // LANES
    tq = _tile(t, 256)
    kern = functools.partial(_ab_prompt_kernel, tq=tq, rs=rs or 2 * tq, n_sb=nu // 2, lam_init=lam_init)
    return pl.pallas_call(
        kern,
        grid=(b, nu, t // tq),
        in_specs=[
            pl.BlockSpec(lam.shape, lambda b_, u, i: (0, 0)),
            pl.BlockSpec((1, LANES), lambda b_, u, i: (0, 0)),
            pl.BlockSpec((2 * tq, tq), lambda b_, u, i: (0, 0)),
            pl.BlockSpec((1, tq, LANES), lambda b_, u, i: (b_, i, u)),
            pl.BlockSpec((1, t, LANES), lambda b_, u, i: (b_, 0, u)),
            pl.BlockSpec((1, t, LANES), lambda b_, u, i: (b_, 0, nu + u)),
        ],
        out_specs=pl.BlockSpec((1, tq, LANES), lambda b_, u, i: (b_, i, u)),
        out_shape=jax.ShapeDtypeStruct((b, t, w), F32),
        compiler_params=_cp("parallel", "parallel", "arbitrary"),
        name="ab_prompt_attn",
    )(lam, subln_g.reshape(1, LANES), _cumsum_matrix(tq), qz, kv, kv)


def _ab_sample_kernel(pt_ref, lam_ref, sg_ref, uu_ref, q_ref, kvn_ref, *rest, npg, page, nq, past, lam_init):
    pages = rest[:npg]
    o_ref = rest[npg]
    qbd_f, qbd, newpage, o_acc, run_s, m_s, l_s = rest[npg + 1:]
    j = pl.program_id(1)
    nj = pl.num_programs(1)
    nu = q_ref.shape[2] // LANES
    rows = nu * 2 * nq
    hr = rows // 2
    kw = nu * LANES

    @pl.when(j == 0)
    def _():
        q = q_ref[0]
        lane_half = lax.broadcasted_iota(jnp.int32, (nq, LANES), 1) // HEAD_DIM
        qbd_f[...] = jnp.zeros_like(qbd_f)
        for u in range(nu):
            qu = q[:, u * LANES:(u + 1) * LANES]
            piece = jnp.concatenate([jnp.where(lane_half == 0, qu, 0.0), jnp.where(lane_half == 1, qu, 0.0)], axis=0)
            qbd_f[u * 2 * nq:(u + 1) * 2 * nq, u * LANES:(u + 1) * LANES] = piece
        qbd[...] = qbd_f[...].astype(BF16)
        newpage[...] = jnp.zeros_like(newpage)
        newpage[0:nq, :] = kvn_ref[0]
        o_acc[...] = jnp.zeros_like(o_acc)
        run_s[...] = jnp.zeros_like(run_s)
        m_s[...] = jnp.full_like(m_s, NEG)
        l_s[...] = jnp.zeros_like(l_s)

    lane = lax.broadcasted_iota(jnp.int32, (hr, page), 1)
    qpos = past + lax.broadcasted_iota(jnp.int32, (hr, page), 0) % nq
    uu = uu_ref[...]

    def process(scores, weighted_values, page_start):
        s = scores(qbd[...]) * QK_SCALE
        kpos = page_start + lane
        z = s[0:hr]
        mask = kpos < qpos
        sp = _softplus(z)
        lk = jnp.where(mask, -sp, 0.0)
        hi, lo = _split_bf16(lk)
        later = _dot(jnp.concatenate([hi, lo], axis=1), uu) + run_s[...]
        w = jnp.where(mask, jnp.exp(z - sp + later), 0.0)
        run_s[...] += jnp.sum(lk, axis=1, keepdims=True)
        sd = s[hr:rows]
        maskd = kpos <= qpos
        sd = jnp.where(maskd, sd, NEG)
        m_old = m_s[...]
        m_new = jnp.maximum(m_old, jnp.max(sd, axis=1, keepdims=True))
        alpha = jnp.exp(m_old - m_new)
        e = jnp.where(maskd, jnp.exp(sd - m_new), 0.0)
        l_s[...] = alpha * l_s[...] + jnp.sum(e, axis=1, keepdims=True)
        m_s[...] = m_new
        pv = weighted_values(jnp.concatenate([w, e], axis=0).astype(BF16))
        o_acc[0:hr, :] += pv[0:hr]
        o_acc[hr:rows, :] = alpha * o_acc[hr:rows, :] + pv[hr:rows]

    @pl.when(j == 0)
    def _():
        process(lambda q: _dot_nt(q, newpage[:, 0:kw].astype(BF16)),
                lambda w: _dot(w, newpage[:, kw:2 * kw].astype(BF16)), past)

    qb = qbd[...]
    s = jnp.concatenate([_dot(qb, pages[p][0, 0].astype(BF16)) for p in range(npg)], axis=1) * QK_SCALE
    z = s[0:hr]
    sp = _softplus(z)
    lk = -sp
    zl = z + lk
    hi, lo = _split_bf16(lk)
    run = run_s[...]
    ws = []
    for p in range(npg):
        cs = slice(p * page, (p + 1) * page)
        later = _dot(jnp.concatenate([hi[:, cs], lo[:, cs]], axis=1), uu) + run
        ws.append(jnp.exp(zl[:, cs] + later))
        run = run + jnp.sum(lk[:, cs], axis=1, keepdims=True)
    run_s[...] = run
    sd = s[hr:rows]
    m_old = m_s[...]
    m_new = jnp.maximum(m_old, jnp.max(sd, axis=1, keepdims=True))
    alpha = jnp.exp(m_old - m_new)
    e = jnp.exp(sd - m_new)
    l_s[...] = alpha * l_s[...] + jnp.sum(e, axis=1, keepdims=True)
    m_s[...] = m_new
    pv = jnp.zeros((rows, kw), F32)
    for p in range(npg):
        we = jnp.concatenate([ws[p], e[:, p * page:(p + 1) * page]], axis=0).astype(BF16)
        pv = pv + _dot_nt(we, pages[p][0, 1].astype(BF16))
    o_acc[0:hr, :] += pv[0:hr]
    o_acc[hr:rows, :] = alpha * o_acc[hr:rows, :] + pv[hr:rows]

    @pl.when(j == nj - 1)
    def _():
        lam = _diff_lambda(lam_ref, lam_init)
        lane_half = lax.broadcasted_iota(jnp.int32, (nq, LANES), 1) // HEAD_DIM
        outs = []
        for u in range(nu):
            blk = o_acc[u * 2 * nq:(u + 1) * 2 * nq, u * LANES:(u + 1) * LANES]
            if u < nu // 2:
                outs.append(jnp.where(lane_half == 0, blk[0:nq], blk[nq:2 * nq]))
            else:
                r0 = u * 2 * nq - hr
                l = l_s[r0:r0 + 2 * nq, :]
                l = jnp.where(l > 0, l, 1.0)
                o = blk[0:nq] / l[0:nq] - lam * (blk[nq:2 * nq] / l[nq:2 * nq])
                outs.append(_rms(o, sg_ref[...]) * (1.0 - lam_init))
        o_ref[0] = jnp.concatenate(outs, axis=1)


def _ab_sample_attn(qz, kv_new, cache, page_table, lam, subln_g, lam_init, npg):
    db, nq, w = qz.shape
    n_pages = page_table.shape[1]
    page = cache.shape[1]
    kvw = kv_new.shape[2]
    nu = w // LANES
    rows = nu * 2 * nq
    past = n_pages * page
    nj = n_pages // npg
    cache_t = jnp.transpose(cache, (0, 2, 3, 4, 1)).reshape(cache.shape[0], 2, kvw // 2, page)
    kern = functools.partial(_ab_sample_kernel, npg=npg, page=page, nq=nq, past=past, lam_init=lam_init)

    def page_spec(p):
        return pl.BlockSpec((1, 2, kvw // 2, page),
                            lambda b, j, pt: (pt[b, n_pages - 1 - (j * npg + p)], 0, 0, 0))

    grid_spec = pltpu.PrefetchScalarGridSpec(
        num_scalar_prefetch=1,
        grid=(db, nj),
        in_specs=[
            pl.BlockSpec(lam.shape, lambda b, j, pt: (0, 0)),
            pl.BlockSpec((1, LANES), lambda b, j, pt: (0, 0)),
            pl.BlockSpec((2 * page, page), lambda b, j, pt: (0, 0)),
            pl.BlockSpec((1, nq, w), lambda b, j, pt: (b, 0, 0)),
            pl.BlockSpec((1, nq, kvw), lambda b, j, pt: (b, 0, 0)),
        ] + [page_spec(p) for p in range(npg)],
        out_specs=pl.BlockSpec((1, nq, w), lambda b, j, pt: (b, 0, 0)),
        scratch_shapes=[
            pltpu.VMEM((rows, w), F32), pltpu.VMEM((rows, w), BF16), pltpu.VMEM((page, kvw), F32),
            pltpu.VMEM((rows, w), F32), pltpu.VMEM((rows // 2, 1), F32),
            pltpu.VMEM((rows // 2, 1), F32), pltpu.VMEM((rows // 2, 1), F32),
        ],
    )
    return pl.pallas_call(
        kern,
        grid_spec=grid_spec,
        out_shape=jax.ShapeDtypeStruct((db, nq, w), F32),
        compiler_params=_cp("parallel", "arbitrary"),
        name="ab_sample_attn",
    )(page_table, lam, subln_g.reshape(1, LANES), _cumsum_matrix(page), qz, kv_new, *([cache_t] * npg))


def _mix_out_kernel(m_ref, x_ref, g_ref, wo_ref, wq_ref, x1_ref, qm_ref):
    y = _dot(m_ref[...].astype(BF16), wo_ref[...])
    x1 = x_ref[...] + _rms(y, g_ref[3:4, :])
    x1_ref[...] = x1
    qm_ref[...] = _dot(_rms(x1, g_ref[4:5, :]).astype(BF16), wq_ref[...])


def _mix_out(m, x, g, w_out, w_q):
    n, d = x.shape
    tm = _tile(n, 512)
    row = lambda i: (i, 0)
    full = lambda i: (0, 0)
    return pl.pallas_call(
        _mix_out_kernel,
        grid=(n // tm,),
        in_specs=[pl.BlockSpec((tm, m.shape[1]), row), pl.BlockSpec((tm, d), row), pl.BlockSpec(g.shape, full),
                  pl.BlockSpec(w_out.shape, full), pl.BlockSpec(w_q.shape, full)],
        out_specs=[pl.BlockSpec((tm, d), row), pl.BlockSpec((tm, w_q.shape[1]), row)],
        out_shape=[jax.ShapeDtypeStruct((n, d), F32), jax.ShapeDtypeStruct((n, w_q.shape[1]), F32)],
        compiler_params=_cp("parallel"),
        name="mix_out",
    )(m, x, g, w_out, w_q)


def _mem_attn_kernel(q_ref, kv_ref, o_ref, *, nh, hd, interleaved):
    q = q_ref[0]
    scale = 1.0 / math.sqrt(hd)
    outs = []
    for h in range(nh):
        qh = q[:, h * hd:(h + 1) * hd].astype(BF16)
        if interleaved:
            mlen = kv_ref.shape[1] // (2 * nh)
            kh = kv_ref[0, pl.ds(h, mlen, stride=2 * nh), :].astype(BF16)
            vh = kv_ref[0, pl.ds(nh + h, mlen, stride=2 * nh), :].astype(BF16)
        else:
            kh = kv_ref[0, :, h * hd:(h + 1) * hd].astype(BF16)
            vh = kv_ref[0, :, (nh + h) * hd:(nh + h + 1) * hd].astype(BF16)
        s = _dot_nt(qh, kh) * scale
        m = jnp.max(s, axis=1, keepdims=True)
        e = jnp.exp(s - m)
        p = e / jnp.sum(e, axis=1, keepdims=True)
        outs.append(_dot(p.astype(BF16), vh))
    o_ref[0] = jnp.concatenate(outs, axis=1)


def _mem_attn(q, kv, nh, interleaved=False):
    b, t, w = q.shape
    tq = _tile(t, 512)
    return pl.pallas_call(
        functools.partial(_mem_attn_kernel, nh=nh, hd=w // nh, interleaved=interleaved),
        grid=(b, t // tq),
        in_specs=[pl.BlockSpec((1, tq, w), lambda b_, i: (b_, i, 0)),
                  pl.BlockSpec((1,) + kv.shape[1:], lambda b_, i: (b_, 0, 0))],
        out_specs=pl.BlockSpec((1, tq, w), lambda b_, i: (b_, i, 0)),
        out_shape=jax.ShapeDtypeStruct((b, t, w), F32),
        compiler_params=_cp("parallel", "arbitrary"),
        name="mem_attn",
    )(q, kv)


def _mem_kv_kernel(x_ref, g_ref, w_ref, o_ref):
    o_ref[0] = _dot(_rms(x_ref[...], g_ref[0]).astype(BF16), w_ref[0])


def _mem_kv(mem, g, w_kv):
    n, d = mem.shape
    nl, _, wo = w_kv.shape
    tm = _tile(n, 512)
    return pl.pallas_call(
        _mem_kv_kernel,
        grid=(nl, n // tm),
        in_specs=[pl.BlockSpec((tm, d), lambda l, i: (i, 0)), pl.BlockSpec((1, 1, d), lambda l, i: (l, 0, 0)),
                  pl.BlockSpec((1, d, wo), lambda l, i: (l, 0, 0))],
        out_specs=pl.BlockSpec((1, tm, wo), lambda l, i: (l, i, 0)),
        out_shape=jax.ShapeDtypeStruct((nl, n, wo), F32),
        compiler_params=_cp("parallel", "parallel"),
        name="mem_kv",
    )(mem, g, w_kv)


def _nsa_proj_kernel(x_ref, g_ref, w_ref, c_ref, s1_ref, s2_ref, qq_ref, prow_ref, wrow_ref, gate_ref, *, qw):
    h = _rms(x_ref[...], g_ref[2:3, :]).astype(BF16)
    c, s1, s2 = c_ref[...], s1_ref[...], s2_ref[...]
    hw = qw // 2
    for i in range(2):
        z = _dot(h, w_ref[:, i * hw:(i + 1) * hw])
        qq_ref[:, i * hw:(i + 1) * hw] = z
        qq_ref[:, qw + i * hw:qw + (i + 1) * hw] = _rope_cols(z, c, s1, s2)
    zr = _dot(h, w_ref[:, qw:qw + 6 * LANES])
    prow_ref[:, 0:2 * LANES] = zr[:, 0:2 * LANES]
    prow_ref[:, 2 * LANES:3 * LANES] = _rope_cols(zr[:, 2 * LANES:3 * LANES], c, s1, s2)
    prow_ref[:, 3 * LANES:4 * LANES] = zr[:, 3 * LANES:4 * LANES]
    wrow_ref[:, 0:LANES] = _rope_cols(zr[:, 4 * LANES:5 * LANES], c, s1, s2)
    wrow_ref[:, LANES:2 * LANES] = zr[:, 5 * LANES:6 * LANES]
    gate_ref[...] = jax.nn.sigmoid(_dot(h, w_ref[:, qw + 6 * LANES:qw + 7 * LANES]))


def _nsa_proj(x, g, w_in, qw, tabs, tab_map, tm):
    n, d = x.shape
    row = lambda i: (i, 0)
    tspec = pl.BlockSpec((tm, LANES), tab_map)
    widths = (2 * qw, 4 * LANES, 2 * LANES, LANES)
    return pl.pallas_call(
        functools.partial(_nsa_proj_kernel, qw=qw),
        grid=(n // tm,),
        in_specs=[pl.BlockSpec((tm, d), row), pl.BlockSpec(g.shape, lambda i: (0, 0)),
                  pl.BlockSpec(w_in.shape, lambda i: (0, 0)), tspec, tspec, tspec],
        out_specs=[pl.BlockSpec((tm, w), row) for w in widths],
        out_shape=[jax.ShapeDtypeStruct((n, w), F32) for w in widths],
        compiler_params=_cp("parallel"),
        name="nsa_proj",
    )(x, g, w_in, *tabs)


def _compress_kernel(xk_ref, xv_ref, pe_ref, wk_ref, wv_ref, o_ref, *, mb):
    acck = jnp.zeros((mb, LANES), F32)
    accv = jnp.zeros((mb, LANES), F32)
    for l in range(BLK):
        ak = xk_ref[pl.ds(l, mb, stride=BLK), :] + pe_ref[l:l + 1, 0:LANES]
        av = xv_ref[pl.ds(l, mb, stride=BLK), :] + pe_ref[l:l + 1, LANES:2 * LANES]
        acck = acck + _dot(ak.astype(BF16), wk_ref[l])
        accv = accv + _dot(av.astype(BF16), wv_ref[l])
    o_ref[:, 0:LANES] = acck
    o_ref[:, LANES:2 * LANES] = accv


def _compress(rows2d, pe2, wk, wv):
    nblk = rows2d.shape[0] // BLK
    mb = _tile(nblk, 128)
    return pl.pallas_call(
        functools.partial(_compress_kernel, mb=mb),
        grid=(nblk // mb,),
        in_specs=[pl.BlockSpec((mb * BLK, LANES), lambda i: (i, 0)),
                  pl.BlockSpec((mb * BLK, LANES), lambda i: (i, 1)),
                  pl.BlockSpec(pe2.shape, lambda i: (0, 0)),
                  pl.BlockSpec(wk.shape, lambda i: (0, 0, 0)),
                  pl.BlockSpec(wv.shape, lambda i: (0, 0, 0))],
        out_specs=pl.BlockSpec((mb, 2 * LANES), lambda i: (i, 0)),
        out_shape=jax.ShapeDtypeStruct((nblk, 2 * LANES), F32),
        compiler_params=_cp("parallel"),
        name="nsa_compress",
    )(rows2d, rows2d, pe2, wk, wv)


def _compress_t_kernel(x_ref, pe_ref, w_ref, o_ref, *, pp):
    for c in range(2):
        acc = jnp.zeros((2 * pp, o_ref.shape[2]), F32)
        for d in range(HEAD_DIM):
            a = jnp.concatenate([x_ref[:, c * LANES + d, :], x_ref[:, c * LANES + HEAD_DIM + d, :]], axis=0)
            a = a + pe_ref[c * HEAD_DIM + d:c * HEAD_DIM + d + 1, :]
            acc = acc + _dot(a.astype(BF16), w_ref[c, d])
        o_ref[2 * c] = acc[0:pp]
        o_ref[2 * c + 1] = acc[pp:2 * pp]


def _compress_pool(cache_t, pe, phi):
    n_pool, _, _, page = cache_t.shape
    bpp = page // BLK
    x = cache_t.reshape(n_pool, 4 * LANES, page)
    pe_t = jnp.tile(jnp.transpose(pe, (0, 2, 1)), (1, 1, bpp)).reshape(2 * HEAD_DIM, page)
    ph = jnp.transpose(phi.reshape(2, BLK, HEAD_DIM, HEAD_DIM), (0, 2, 1, 3))
    eye = jnp.eye(bpp, dtype=F32)
    w = jnp.einsum('ab,cdle->cdalbe', eye, ph).reshape(2, HEAD_DIM, page, bpp * HEAD_DIM).astype(BF16)
    pp = _tile(n_pool, 128)
    out = pl.pallas_call(
        functools.partial(_compress_t_kernel, pp=pp),
        grid=(n_pool // pp,),
        in_specs=[pl.BlockSpec((pp, 2 * LANES, page), lambda i: (i, 0, 0)),
                  pl.BlockSpec(pe_t.shape, lambda i: (0, 0)),
                  pl.BlockSpec(w.shape, lambda i: (0, 0, 0, 0))],
        out_specs=pl.BlockSpec((4, pp, bpp * HEAD_DIM), lambda i: (0, i, 0)),
        out_shape=jax.ShapeDtypeStruct((4, n_pool, bpp * HEAD_DIM), F32),
        compiler_params=_cp("parallel"),
        name="nsa_compress_pool",
    )(x, pe_t, w)
    out = out.reshape(2, 2, n_pool, bpp, HEAD_DIM).transpose(2, 3, 0, 1, 4)
    return out.reshape(n_pool, bpp, 4 * HEAD_DIM)


def _compress_weights(pe, phi):
    pe2 = jnp.concatenate([pe[0], pe[0], pe[1], pe[1]], axis=1)
    ph = phi.reshape(2, BLK, HEAD_DIM, HEAD_DIM)
    z = jnp.zeros_like(ph[0])
    def bd(p):
        return jnp.concatenate([jnp.concatenate([p, z], axis=2), jnp.concatenate([z, p], axis=2)], axis=1).astype(BF16)
    return pe2, bd(ph[0]), bd(ph[1])


def _select_blocks(imp, nb):
    lane = lax.broadcasted_iota(jnp.int32, imp.shape, 1)
    cnt = jnp.zeros(imp.shape, F32)
    for i in range(nb):
        ci = imp[:, i:i + 1]
        ahead = (ci > imp) | ((ci == imp) & (lane > i))
        cnt = cnt + jnp.where(ahead, 1.0, 0.0)
    return jnp.where(cnt < float(min(TOPK, nb)), 1.0, 0.0)


def _masked_softmax(s, mask):
    s = jnp.where(mask, s, NEG)
    m = jnp.max(s, axis=-1, keepdims=True)
    e = jnp.where(mask, jnp.exp(s - m), 0.0)
    den = jnp.sum(e, axis=-1, keepdims=True)
    return e / jnp.where(den > 0, den, 1.0)


def _place(x, g):
    z = jnp.zeros_like(x)
    return jnp.concatenate([x, z] if g == 0 else [z, x], axis=1)


def _nsa_prompt_kernel(qq_ref, gate_ref, cmp_ref, e_ref, sv_ref, wv_ref, o_ref, bias_scr,
                       *, tq, kc, nb, nheads, wk):
    qi = pl.program_id(1)
    qw = nheads * HEAD_DIM
    hg = nheads // 2
    t = qi * tq + lax.broadcasted_iota(jnp.int32, (tq, 1), 0)
    qq = qq_ref[0] * QK_SCALE
    gates = gate_ref[0]
    cmp = cmp_ref[0]
    ck = cmp[:, 0:LANES].astype(BF16)
    cv = cmp[:, LANES:2 * LANES].astype(BF16)
    blk = lax.broadcasted_iota(jnp.int32, (tq, nb), 1)
    done = (blk + 1) * BLK - 1 <= t
    cur = t // BLK
    forced = (blk == cur) | (blk == 0)
    future = blk > cur
    nchunks = (qi * tq + tq - 1) // kc + 1
    wstart = pl.multiple_of(jnp.maximum(qi * tq - WINDOW, 0), tq)
    kwin = wv_ref[0, pl.ds(wstart, wk), 0:LANES].astype(BF16)
    vwin = wv_ref[0, pl.ds(wstart, wk), LANES:2 * LANES].astype(BF16)
    wpos = wstart + lax.broadcasted_iota(jnp.int32, (tq, wk), 1)
    wmask = (wpos <= t) & (wpos > t - WINDOW)
    kpos_c = lax.broadcasted_iota(jnp.int32, (tq, kc), 1)

    rows = hg * tq
    outs = []
    for g in range(2):
        def stack(off):
            return jnp.concatenate(
                [_place(qq[:, off + (g * hg + h8) * HEAD_DIM:off + (g * hg + h8 + 1) * HEAD_DIM], g)
                 for h8 in range(hg)], axis=0).astype(BF16)

        q = stack(0)
        qr = stack(qw)
        gcols = jnp.concatenate([gates[:, 3 * (g * hg + h8):3 * (g * hg + h8) + 3] for h8 in range(hg)], axis=0)

        s = _dot_nt(q, ck).reshape(hg, tq, nb)
        p = _masked_softmax(s, done[None])
        imp = jnp.sum(p, axis=0)
        o_cmp = _dot(p.reshape(rows, nb).astype(BF16), cv)
        imp = jnp.where(forced, FORCE, jnp.where(future, -1.0, imp))
        sel = _select_blocks(imp, nb).astype(BF16)

        def fill(c, _):
            allowed = (_dot(sel, e_ref[c]) > 0.5) & (c * kc + kpos_c <= t)
            bias_scr[c] = jnp.where(allowed, 0.0, NEG)
            return 0
        lax.fori_loop(0, nchunks, fill, 0)

        def body(c, carry):
            m, l, acc = carry
            start = pl.multiple_of(c * kc, kc)
            ks = sv_ref[0, pl.ds(start, kc), 0:LANES].astype(BF16)
            vs = sv_ref[0, pl.ds(start, kc), LANES:2 * LANES].astype(BF16)
            s = _dot_nt(qr, ks).reshape(hg, tq, kc) + bias_scr[c][None]
            s = s.reshape(rows, kc)
            m_new = jnp.maximum(m, jnp.max(s, axis=1, keepdims=True))
            alpha = jnp.exp(m - m_new)
            e = jnp.exp(s - m_new)
            l = alpha * l + jnp.sum(e, axis=1, keepdims=True)
            acc = alpha * acc + _dot(e.astype(BF16), vs)
            return m_new, l, acc

        _, l, acc = lax.fori_loop(
            0, nchunks, body,
            (jnp.full((rows, 1), NEG, F32), jnp.zeros((rows, 1), F32), jnp.zeros((rows, LANES), F32)))
        o_slc = acc / l

        sw = _dot_nt(qr, kwin).reshape(hg, tq, wk)
        pw = _masked_softmax(sw, wmask[None]).reshape(rows, wk)
        o_win = _dot(pw.astype(BF16), vwin)

        o = gcols[:, 0:1] * o_cmp + gcols[:, 1:2] * o_slc + gcols[:, 2:3] * o_win
        for h8 in range(hg):
            outs.append(o[h8 * tq:(h8 + 1) * tq, g * HEAD_DIM:(g + 1) * HEAD_DIM])
    o_ref[0] = jnp.concatenate(outs, axis=1)


def _block_expand(nb_rows, nkeys, kc):
    key = jnp.arange(nkeys).reshape(nkeys // kc, 1, kc)
    return (key // BLK == jnp.arange(nb_rows)[None, :, None]).astype(BF16)


def _nsa_prompt_attn(qq, gates, cmp, prow, wrow, nheads):
    b, t, _ = qq.shape
    qw = nheads * HEAD_DIM
    tq = _tile(t, 128)
    kc = _tile(t, 512)
    nb = t // BLK
    wk = min(t, tq + WINDOW)
    e = _block_expand(nb, t, kc)
    kern = functools.partial(_nsa_prompt_kernel, tq=tq, kc=kc, nb=nb, nheads=nheads, wk=wk)
    return pl.pallas_call(
        kern,
        grid=(b, t // tq),
        in_specs=[
            pl.BlockSpec((1, tq, 2 * qw), lambda b_, i: (b_, i, 0)),
            pl.BlockSpec((1, tq, LANES), lambda b_, i: (b_, i, 0)),
            pl.BlockSpec((1, nb, 2 * LANES), lambda b_, i: (b_, 0, 0)),
            pl.BlockSpec(e.shape, lambda b_, i: (0, 0, 0)),
            pl.BlockSpec((1, t, 2 * LANES), lambda b_, i: (b_, 0, 1)),
            pl.BlockSpec((1, t, 2 * LANES), lambda b_, i: (b_, 0, 0)),
        ],
        out_specs=pl.BlockSpec((1, tq, qw), lambda b_, i: (b_, i, 0)),
        out_shape=jax.ShapeDtypeStruct((b, t, qw), F32),
        scratch_shapes=[pltpu.VMEM((t // kc, tq, kc), F32)],
        compiler_params=_cp("parallel", "arbitrary"),
        name="nsa_prompt_attn",
    )(qq, gates, cmp, e, prow, wrow)


def _nsa_sample_kernel(pt_ref, q_ref, qr_ref, gate_ref, cnew_ref, e_ref, new_ref, win_ref, wnew_ref,
                       *rest, n_pages, page, nq, hg, nbp, past):
    pages = rest[:n_pages]
    cmps = rest[n_pages:2 * n_pages]
    o_ref = rest[2 * n_pages]
    cmp_scr = rest[2 * n_pages + 1]
    rows = 2 * nq * hg
    bpp = page // BLK
    nb = n_pages * bpp + 1
    nnew = new_ref.shape[1]
    nwin = win_ref.shape[3]

    cmp_scr[...] = jnp.zeros_like(cmp_scr)
    for p in range(n_pages):
        cmp_scr[p * bpp:(p + 1) * bpp, :] = cmps[p][0]
    cmp_scr[n_pages * bpp:n_pages * bpp + 1, :] = cnew_ref[0, 0:1, :]
    cmp = cmp_scr[...]
    ck = cmp[:, 0:LANES].astype(BF16)
    cv = cmp[:, LANES:2 * LANES].astype(BF16)

    q = q_ref[0].astype(BF16)
    qr = qr_ref[0].astype(BF16)
    gates = gate_ref[0]
    rowi = lax.broadcasted_iota(jnp.int32, (rows, 1), 0)
    t = past + (rowi // hg) % nq
    top = rowi < rows // 2

    def pick(x):
        return jnp.where(top, x[:, 0:HEAD_DIM], x[:, HEAD_DIM:2 * HEAD_DIM])

    blk = lax.broadcasted_iota(jnp.int32, (rows, nbp), 1)
    done = ((blk + 1) * BLK - 1 <= t) & (blk < nb)
    p_cmp = _masked_softmax(_dot_nt(q, ck) * QK_SCALE, done)
    o_cmp = pick(_dot(p_cmp.astype(BF16), cv))

    ng = rows // hg
    imp = jnp.sum(p_cmp.reshape(ng, hg, nbp), axis=1)
    tg = past + lax.broadcasted_iota(jnp.int32, (ng, 1), 0) % nq
    blk_g = lax.broadcasted_iota(jnp.int32, (ng, nbp), 1)
    cur = tg // BLK
    imp = jnp.where((blk_g == cur) | (blk_g == 0), FORCE, jnp.where(blk_g > cur, -1.0, imp))
    sel = _select_blocks(imp, nb)
    expand = (lax.broadcasted_iota(jnp.int32, (rows, ng), 0) // hg
              == lax.broadcasted_iota(jnp.int32, (rows, ng), 1)).astype(F32)
    sel_f = _dot(expand, sel)
    sel_rows = sel_f.astype(BF16)

    allowed = _dot(sel_rows, e_ref[...]) > 0.5
    s_parts = [_dot(qr, pages[p][0, 0].astype(BF16)) for p in range(n_pages)]
    s = jnp.concatenate(s_parts, axis=1) * QK_SCALE
    s = jnp.where(allowed, s, NEG)
    new = new_ref[0]
    s_n = _dot_nt(qr, new[:, 0:LANES].astype(BF16)) * QK_SCALE
    npos = past + lax.broadcasted_iota(jnp.int32, (rows, nnew), 1)
    mask_n = (npos <= t) & (sel_f[:, nb - 1:nb] > 0.5)
    s_n = jnp.where(mask_n, s_n, NEG)
    m = jnp.maximum(jnp.max(s, axis=1, keepdims=True), jnp.max(s_n, axis=1, keepdims=True))
    e = jnp.where(allowed, jnp.exp(s - m), 0.0)
    e_n = jnp.where(mask_n, jnp.exp(s_n - m), 0.0)
    den = jnp.sum(e, axis=1, keepdims=True) + jnp.sum(e_n, axis=1, keepdims=True)
    acc = _dot(e_n.astype(BF16), new[:, LANES:2 * LANES].astype(BF16))
    for p in range(n_pages):
        acc = acc + _dot_nt(e[:, p * page:(p + 1) * page].astype(BF16), pages[p][0, 1].astype(BF16))
    o_slc = pick(acc / jnp.where(den > 0, den, 1.0))

    wnew = wnew_ref[0]
    s_w = _dot(qr, win_ref[0, 0].astype(BF16)) * QK_SCALE
    wpos = past - nwin + lax.broadcasted_iota(jnp.int32, (rows, nwin), 1)
    mask_w = (wpos <= t) & (wpos > t - WINDOW) & (wpos >= 0)
    s_w = jnp.where(mask_w, s_w, NEG)
    s_wn = _dot_nt(qr, wnew[:, 0:LANES].astype(BF16)) * QK_SCALE
    mask_wn = (npos <= t) & (npos > t - WINDOW)
    s_wn = jnp.where(mask_wn, s_wn, NEG)
    m = jnp.maximum(jnp.max(s_w, axis=1, keepdims=True), jnp.max(s_wn, axis=1, keepdims=True))
    e = jnp.where(mask_w, jnp.exp(s_w - m), 0.0)
    e_n = jnp.where(mask_wn, jnp.exp(s_wn - m), 0.0)
    den = jnp.sum(e, axis=1, keepdims=True) + jnp.sum(e_n, axis=1, keepdims=True)
    acc = _dot_nt(e.astype(BF16), win_ref[0, 1].astype(BF16)) \
        + _dot(e_n.astype(BF16), wnew[:, LANES:2 * LANES].astype(BF16))
    o_win = pick(acc / jnp.where(den > 0, den, 1.0))

    o_ref[0] = gates[:, 0:1] * o_cmp + gates[:, 1:2] * o_slc + gates[:, 2:3] * o_win


def _nsa_sample_attn(q, q_rot, gates, cmp_pool, cmp_new, cache, new_blk, win_state, wnew, page_table, nq):
    db, rows, _ = q.shape
    n_pages = page_table.shape[1]
    page = cache.shape[3]
    past = n_pages * page
    nb = n_pages * (page // BLK) + 1
    nbp = -(-nb // 8) * 8
    nkeys = n_pages * page
    e = (jnp.arange(nkeys)[None, :] // BLK == jnp.arange(nbp)[:, None]).astype(BF16)
    nnew = new_blk.shape[1]
    kern = functools.partial(_nsa_sample_kernel, n_pages=n_pages, page=page, nq=nq, hg=rows // (2 * nq),
                             nbp=nbp, past=past)

    bpp = page // BLK
    cmp_pool3 = cmp_pool

    def page_spec(p):
        return pl.BlockSpec((1, 2, LANES, page), lambda b, pt: (pt[b, p], 1, 0, 0))

    def cmp_spec(p):
        return pl.BlockSpec((1, bpp, 2 * LANES), lambda b, pt: (pt[b, p], 0, 0))

    grid_spec = pltpu.PrefetchScalarGridSpec(
        num_scalar_prefetch=1,
        grid=(db,),
        in_specs=[
            pl.BlockSpec((1, rows, LANES), lambda b, pt: (b, 0, 0)),
            pl.BlockSpec((1, rows, LANES), lambda b, pt: (b, 0, 0)),
            pl.BlockSpec((1, rows, 3), lambda b, pt: (b, 0, 0)),
            pl.BlockSpec((1, 1, 2 * LANES), lambda b, pt: (b, 0, 0)),
            pl.BlockSpec(e.shape, lambda b, pt: (0, 0)),
            pl.BlockSpec((1, nnew, 2 * LANES), lambda b, pt: (b, 0, 1)),
            pl.BlockSpec((1,) + win_state.shape[1:], lambda b, pt: (b, 0, 0, 0)),
            pl.BlockSpec((1, nnew, 2 * LANES), lambda b, pt: (b, 0, 0)),
        ] + [page_spec(p) for p in range(n_pages)] + [cmp_spec(p) for p in range(n_pages)],
        out_specs=pl.BlockSpec((1, rows, HEAD_DIM), lambda b, pt: (b, 0, 0)),
        scratch_shapes=[pltpu.VMEM((nbp, 2 * LANES), F32)],
    )
    return pl.pallas_call(
        kern,
        grid_spec=grid_spec,
        out_shape=jax.ShapeDtypeStruct((db, rows, HEAD_DIM), F32),
        compiler_params=_cp("arbitrary"),
        name="nsa_sample_attn",
    )(page_table, q, q_rot, gates, cmp_new, e, new_blk, win_state, wnew,
      *([cache] * n_pages), *([cmp_pool3] * n_pages))


def kernel(x_prompt, x_sample, mem_prompt, page_table, cache_kv_l0, cache_mem_l0, cache_nsa_l1, state_win_l1,
           cache_mem_l1, cache_kv_l2, cache_mem_l2, cache_nsa_l3, state_win_l3, cache_mem_l3, norm_g, mem_norm_g,
           ffn_w_in, ffn_w_out, w_q_mem, w_kv_mem, w_o_mem, ab_w_in, ab_w_out, diff_lambda, diff_subln_g,
           nsa_w_in, nsa_w_out, nsa_cmp_pe, nsa_cmp_phi):
    bsz, seq, d = x_prompt.shape
    db, dq, _ = x_sample.shape
    depth = norm_g.shape[0]
    n_pages = page_table.shape[1]
    page = cache_kv_l0.shape[1]
    past = n_pages * page
    mlen = mem_prompt.shape[1]
    mem_w = w_q_mem.shape[2]
    mem_heads = mem_w // LANES
    nheads = d // HEAD_DIM
    qw = nheads * HEAD_DIM
    kv_caches = (cache_kv_l0, cache_kv_l2)
    nsa_caches = (cache_nsa_l1, cache_nsa_l3)
    win_states = (state_win_l1, state_win_l3)
    mem_caches = (cache_mem_l0, cache_mem_l1, cache_mem_l2, cache_mem_l3)

    np_, ns_ = bsz * seq, db * dq
    tm_p = _tile(seq, 512)
    tm_s = ns_
    tabs_p = _rope_tables(jnp.arange(seq, dtype=jnp.int32))
    tabs_s = _rope_tables(past + jnp.arange(ns_, dtype=jnp.int32) % dq)
    spt = seq // tm_p
    map_p = lambda i: (i % spt, 0)
    map_s = lambda i: (0, 0)

    bf = lambda w: w.astype(BF16)
    ffn_in, ffn_out = _ffn_chunks(bf(ffn_w_in), ffn_w_out.shape[2]), bf(ffn_w_out)
    wq_m, wkv_m, wo_m = bf(w_q_mem), bf(w_kv_mem), bf(w_o_mem)
    ab_in, ab_out = bf(ab_w_in), bf(ab_w_out)
    nsa_in = bf(jnp.pad(nsa_w_in, ((0, 0), (0, 0), (0, qw + 7 * LANES - nsa_w_in.shape[2]))))
    nsa_out = bf(nsa_w_out)

    mkv_all = _mem_kv(mem_prompt.reshape(bsz * mlen, d), mem_norm_g.reshape(depth, 1, d), wkv_m)

    xp = x_prompt.reshape(np_, d)
    xs = x_sample.reshape(ns_, d)
    kv_p, kv_s, nsa_p, nsa_s, win_p, win_s, mem_p = [], [], [], [], [], [], []
    om_p = om_s = None
    for layer in range(depth):
        g = norm_g[layer]
        i = layer // 2
        if layer == 0:
            xp = _ffn_half(xp, g, ffn_in[layer, 0], ffn_out[layer, 0], 0, 1)
            xs = _ffn_half(xs, g, ffn_in[layer, 0], ffn_out[layer, 0], 0, 1)
        if layer % 2 == 0:
            lam_init = 0.8 - 0.6 * math.exp(-0.3 * layer)
            lam = diff_lambda[i]
            sg = diff_subln_g[i]
            qz_p, kvr_p = _ab_proj(xp, g, ab_in[i], tabs_p, map_p, tm_p)
            qz_s, kvr_s = _ab_proj(xs, g, ab_in[i], tabs_s, map_s, tm_s)
            kvw = kvr_p.shape[1]
            mp = _ab_prompt_attn(qz_p.reshape(bsz, seq, -1), kvr_p.reshape(bsz, seq, kvw), lam, sg, lam_init)
            ms = _ab_sample_attn(qz_s.reshape(db, dq, -1), kvr_s.reshape(db, dq, kvw), kv_caches[i], page_table,
                                 lam, sg, lam_init, npg=_tile(n_pages, 16))
            kv_p.append(kvr_p.reshape(bsz, seq, 2, kvw // (2 * HEAD_DIM), HEAD_DIM))
            kv_s.append(kvr_s.reshape(db, dq, 2, kvw // (2 * HEAD_DIM), HEAD_DIM))
            mp = mp.reshape(np_, -1)
            ms = ms.reshape(ns_, -1)
            w_mix = ab_out[i]
        else:
            qq_p, prow_p, wrow_p, gate_p = _nsa_proj(xp, g, nsa_in[i], qw, tabs_p, map_p, tm_p)
            qq_s, prow_s, wrow_s, gate_s = _nsa_proj(xs, g, nsa_in[i], qw, tabs_s, map_s, tm_s)
            pe2, wk, wv = _compress_weights(nsa_cmp_pe[i], nsa_cmp_phi[i])
            cmp_p = _compress(prow_p, pe2, wk, wv).reshape(bsz, seq // BLK, 2 * LANES)
            mp = _nsa_prompt_attn(qq_p.reshape(bsz, seq, -1), gate_p.reshape(bsz, seq, LANES), cmp_p,
                                  prow_p.reshape(bsz, seq, -1), wrow_p.reshape(bsz, seq, -1), nheads)
            mp = mp.reshape(np_, -1)
            cache = nsa_caches[i]
            n_pool = cache.shape[0]
            roww = prow_s.shape[1]
            cache_t = jnp.transpose(cache, (0, 2, 3, 4, 1)).reshape(n_pool, 4, LANES, page)
            cmp_pool = _compress_pool(cache_t, nsa_cmp_pe[i], nsa_cmp_phi[i])
            new_blk = jnp.pad(prow_s.reshape(db, dq, roww), ((0, 0), (0, BLK - dq), (0, 0)))
            cmp_new = _compress(new_blk.reshape(db * BLK, roww), pe2, wk, wv).reshape(db, 1, 2 * LANES)
            wnew = jnp.pad(wrow_s.reshape(db, dq, -1), ((0, 0), (0, BLK - dq), (0, 0)))
            nwin = win_states[i].shape[1]
            win_state = jnp.transpose(win_states[i], (0, 2, 3, 4, 1)).reshape(db, 2, LANES, nwin)
            hg = nheads // 2

            def to_rows(a, pad):
                w = a.shape[1] // nheads
                a = a.reshape(db, dq, 2, hg, w).transpose(0, 2, 1, 3, 4)
                if pad:
                    z = jnp.zeros_like(a[:, 0:1])
                    a = jnp.concatenate([jnp.concatenate([a[:, 0:1], z], axis=-1),
                                         jnp.concatenate([z, a[:, 1:2]], axis=-1)], axis=1)
                return a.reshape(db, 2 * dq * hg, -1)

            q_rows = to_rows(qq_s[:, 0:qw], True)
            qr_rows = to_rows(qq_s[:, qw:2 * qw], True)
            g_rows = to_rows(gate_s[:, 0:3 * nheads], False)
            o_rows = _nsa_sample_attn(q_rows, qr_rows, g_rows, cmp_pool, cmp_new, cache_t, new_blk, win_state, wnew,
                                      page_table, dq)
            ms = o_rows.reshape(db, 2, dq, hg, HEAD_DIM).transpose(0, 2, 1, 3, 4).reshape(ns_, qw)
            nsa_p.append(prow_p.reshape(bsz, seq, 4, 2, HEAD_DIM))
            nsa_s.append(prow_s.reshape(db, dq, 4, 2, HEAD_DIM))
            n_keep = min(WINDOW, seq)
            win_p.append(wrow_p.reshape(bsz, seq, 2, 2, HEAD_DIM)[:, seq - n_keep:])
            win_all = jnp.concatenate([win_states[i], wrow_s.reshape(db, dq, 2, 2, HEAD_DIM)], axis=1)
            win_s.append(win_all[:, dq:])
            w_mix = nsa_out[i]

        xp, qm_p = _mix_out(mp, xp, g, w_mix, wq_m[layer])
        xs, qm_s = _mix_out(ms, xs, g, w_mix, wq_m[layer])
        mkv = mkv_all[layer].reshape(bsz, mlen, 2 * mem_w)
        om_p = _mem_attn(qm_p.reshape(bsz, seq, mem_w), mkv, mem_heads).reshape(np_, mem_w)
        om_s = _mem_attn(qm_s.reshape(db, dq, mem_w), mem_caches[layer].reshape(db, mlen * 2 * mem_heads, LANES),
                         mem_heads, interleaved=True).reshape(ns_, mem_w)
        mem_p.append(mkv.reshape(bsz, mlen, 2, mem_heads, LANES))
        xp = _ffn_half(xp, g, ffn_in[layer, 1], ffn_out[layer, 1], 6, 7, om_p, wo_m[layer], 5)
        xs = _ffn_half(xs, g, ffn_in[layer, 1], ffn_out[layer, 1], 6, 7, om_s, wo_m[layer], 5)
        if layer + 1 < depth:
            gn = norm_g[layer + 1]
            xp = _ffn_half(xp, gn, ffn_in[layer + 1, 0], ffn_out[layer + 1, 0], 0, 1)
            xs = _ffn_half(xs, gn, ffn_in[layer + 1, 0], ffn_out[layer + 1, 0], 0, 1)

    return (xp.reshape(bsz, seq, d), xs.reshape(db, dq, d),
            kv_p[0], kv_s[0], mem_p[0],
            nsa_p[0], nsa_s[0], win_p[0], win_s[0], mem_p[1],
            kv_p[1], kv_s[1], mem_p[2],
            nsa_p[1], nsa_s[1], win_p[1], win_s[1], mem_p[3])
```
